```python
import jax
import jax.numpy as jnp
from jax import lax
import numpy as np

D_MODEL = 2048
BATCH = 2
SEQ = 4096
DEPTH = 2
DEC_BATCH = 8
DEC_SEQ = 8
PAST_LEN = 16384
PAGE_SIZE = 128

N_EVEN = (DEPTH + 1) // 2
N_ODD = DEPTH // 2
SB_HEADS = 8
SB_HEAD_DIM = 128
SB_WIDTH = SB_HEADS * SB_HEAD_DIM
SB_BLOCK = 128
SB_BIAS_INIT = -6.0
GLA_HEADS = 4
GLA_DK = 128
GLA_DV = 256
GLA_K_WIDTH = GLA_HEADS * GLA_DK
GLA_V_WIDTH = GLA_HEADS * GLA_DV
GLA_GATE_RANK = 16
GLA_GATE_NORMALIZER = 16.0
GLA_CHUNK = 64
GLA_NORM_EPS = 1e-5
EVEN_SPLITS = [SB_WIDTH, 2 * SB_WIDTH, 3 * SB_WIDTH,
               3 * SB_WIDTH + GLA_K_WIDTH, 3 * SB_WIDTH + 2 * GLA_K_WIDTH,
               3 * SB_WIDTH + 2 * GLA_K_WIDTH + GLA_V_WIDTH, 3 * SB_WIDTH + 2 * GLA_K_WIDTH + 2 * GLA_V_WIDTH]
EVEN_IN_WIDTH = 3 * SB_WIDTH + 2 * GLA_K_WIDTH + 2 * GLA_V_WIDTH + GLA_GATE_RANK
EVEN_OUT_WIDTH = SB_WIDTH + GLA_V_WIDTH
RW_HEAD = 64
RW_HEADS = D_MODEL // RW_HEAD
RW_DECAY_RANK = 96
RW_A_RANK = 96
RW_GATE_RANK = 256
RW_LN_EPS = 64e-5
N_MEM = 256
MEM_HEADS = 4
MEM_HEAD_DIM = 128
MEM_WIDTH = MEM_HEADS * MEM_HEAD_DIM
FFN_HIDDEN = 5504
CONV_WIDTH = 3
LN_EPS = 1e-5
DEEPNORM_ALPHA = (2.0 * DEPTH) ** 0.25
DEEPNORM_BETA = (8.0 * DEPTH) ** -0.25

kernel_name = 'hybrid_sb_gla_rwkv7_convffn_decoder_step'


def layer_norm(x, w, b):
    xf = x.astype(jnp.float32)
    mu = jnp.mean(xf, axis=-1, keepdims=True)
    var = jnp.mean(jnp.square(xf - mu), axis=-1, keepdims=True)
    return ((xf - mu) * lax.rsqrt(var + LN_EPS) * w + b).astype(x.dtype)


def stick_breaking(q, k, v, bias, q_pos, k_pos):
    z = jnp.einsum('bthd,bshd->bhts', q, k).astype(jnp.float32) * (SB_HEAD_DIM ** -0.5)
    z = z + bias.astype(jnp.float32)[None, :, None, None]
    vis = k_pos[None, :] < q_pos[:, None]
    log_beta = jax.nn.log_sigmoid(z)
    log_keep = jnp.where(vis, jax.nn.log_sigmoid(-z), 0.0)
    tail = lax.cumsum(log_keep, axis=3, reverse=True) - log_keep
    a = jnp.where(vis, jnp.exp(log_beta + tail), 0.0)
    return jnp.einsum('bhts,bshd->bthd', a.astype(v.dtype), v)


def sb_prompt(q, k, v, bias):
    B, T, H, Dh = q.shape
    n_blk = T // SB_BLOCK
    k_pos = jnp.arange(T)
    q_blocks = q.reshape(B, n_blk, SB_BLOCK, H, Dh).transpose(1, 0, 2, 3, 4)

    def one_block(args):
        q_blk, blk = args
        q_pos = blk * SB_BLOCK + jnp.arange(SB_BLOCK)
        return stick_breaking(q_blk, k, v, bias, q_pos, k_pos)

    out = lax.map(one_block, (q_blocks, jnp.arange(n_blk)))
    return out.transpose(1, 0, 2, 3, 4).reshape(B, T, H, Dh)


def gla_chunked(q, k, v, log_a, s0):
    B, T, H, dk = q.shape
    dv = v.shape[-1]
    C = GLA_CHUNK if T % GLA_CHUNK == 0 else T
    n = T // C
    f32 = jnp.float32
    qc = q.astype(f32).reshape(B, n, C, H, dk) * (dk ** -0.5)
    kc = k.astype(f32).reshape(B, n, C, H, dk)
    vc = v.astype(f32).reshape(B, n, C, H, dv)
    b = jnp.cumsum(log_a.astype(f32).reshape(B, n, C, H, dk), axis=2)
    b_last = b[:, :, -1]
    q_dec = qc * jnp.exp(b)
    scores = jnp.einsum('bnthk,bnshk->bnhts', q_dec, kc * jnp.exp(-b))
    causal = jnp.tril(jnp.ones((C, C), dtype=bool))
    o_intra = jnp.einsum('bnhts,bnshv->bnthv', jnp.where(causal, scores, 0.0), vc)
    d_state = jnp.einsum('bnshk,bnshv->bnhkv', kc * jnp.exp(b_last[:, :, None] - b), vc)
    decay = jnp.exp(b_last)

    def step(S, inp):
        d, ds = inp
        return S * d[..., None] + ds, S

    s_new, s_in = lax.scan(step, s0.astype(f32), (decay.transpose(1, 0, 2, 3), d_state.transpose(1, 0, 2, 3, 4)))
    o_inter = jnp.einsum('bnthk,bnhkv->bnthv', q_dec, s_in.transpose(1, 0, 2, 3, 4))
    return (o_intra + o_inter).reshape(B, T, H, dv), s_new


def even_mixer(x, past_k, past_v, s0, W, j):
    B, T, _ = x.shape
    qa, ka, va, qb, kb, vb, gb, low = jnp.split(x @ W['mx_w_in'][j], EVEN_SPLITS, axis=-1)
    qa = qa.reshape(B, T, SB_HEADS, SB_HEAD_DIM)
    ka = ka.reshape(B, T, SB_HEADS, SB_HEAD_DIM)
    va = va.reshape(B, T, SB_HEADS, SB_HEAD_DIM)
    bias = W['sb_bias'][j]
    if past_k is None:
        oa = sb_prompt(qa, ka, va, bias)
    else:
        P = past_k.shape[1]
        keys = jnp.concatenate([past_k.astype(ka.dtype), ka], axis=1)
        vals = jnp.concatenate([past_v.astype(va.dtype), va], axis=1)
        oa = stick_breaking(qa, keys, vals, bias, P + jnp.arange(T), jnp.arange(P + T))
    log_a = jax.nn.log_sigmoid((low @ W['gla_w_a2'][j] + W['gla_b_a'][j]).astype(jnp.float32)) / GLA_GATE_NORMALIZER
    ob, s_new = gla_chunked(qb.reshape(B, T, GLA_HEADS, GLA_DK), kb.reshape(B, T, GLA_HEADS, GLA_DK),
                            vb.reshape(B, T, GLA_HEADS, GLA_DV), log_a.reshape(B, T, GLA_HEADS, GLA_DK), s0)
    ob = ob * lax.rsqrt(jnp.mean(jnp.square(ob), axis=-1, keepdims=True) + GLA_NORM_EPS) * W['gla_norm_w'][j]
    ob = ob * jax.nn.silu(gb.reshape(B, T, GLA_HEADS, GLA_DV).astype(jnp.float32))
    o = jnp.concatenate([oa.reshape(B, T, SB_WIDTH), ob.reshape(B, T, GLA_V_WIDTH).astype(x.dtype)], axis=-1)
    return o @ W['mx_w_out'][j], ka, va, s_new


def rwkv7_mixer(x, s0, shift0, W, j):
    B, T, D = x.shape
    H, N = RW_HEADS, RW_HEAD
    f32 = jnp.float32
    mix = W['rw_mix'][j]
    x_prev = jnp.concatenate([shift0[:, None, :].astype(x.dtype), x[:, :-1]], axis=1)
    xx = x_prev - x
    xr, xw, xk, xv, xa, xg = [x + xx * mix[m] for m in range(6)]
    w_rkv = W['rw_w_rkv'][j]
    r = (xr @ w_rkv[0]).astype(f32).reshape(B, T, H, N)
    k = (xk @ w_rkv[1]).astype(f32).reshape(B, T, H, N)
    v = (xv @ w_rkv[2]).astype(f32).reshape(B, T, H, N)
    w_log = -jax.nn.softplus(-(W['rw_w0'][j] + jnp.tanh(xw @ W['rw_w1'][j]) @ W['rw_w2'][j]).astype(f32)) - 0.5
    decay = jnp.exp(-jnp.exp(w_log)).reshape(B, T, H, N)
    a = jax.nn.sigmoid((W['rw_a0'][j] + (xa @ W['rw_a1'][j]) @ W['rw_a2'][j]).astype(f32)).reshape(B, T, H, N)
    g = (jax.nn.sigmoid(xg @ W['rw_g1'][j]) @ W['rw_g2'][j]).astype(f32)
    kk = k * W['rw_k_k'][j].reshape(H, N)
    kk = kk * lax.rsqrt(jnp.maximum(jnp.sum(jnp.square(kk), axis=-1, keepdims=True), 1e-24))
    k = k * (1.0 + (a - 1.0) * W['rw_k_a'][j].reshape(H, N))

    def step(S, inp):
        r_t, w_t, k_t, v_t, kk_t, a_t = inp
        sa = jnp.einsum('bhvk,bhk->bhv', S, -kk_t)
        S = S * w_t[:, :, None, :] + sa[..., None] * (kk_t * a_t)[:, :, None, :] + v_t[..., None] * k_t[:, :, None, :]
        return S, jnp.einsum('bhvk,bhk->bhv', S, r_t)

    seq = tuple(t.transpose(1, 0, 2, 3) for t in (r, decay, k, v, kk, a))
    s_new, y = lax.scan(step, s0.astype(f32), seq)
    y = y.transpose(1, 0, 2, 3)
    mu = jnp.mean(y, axis=-1, keepdims=True)
    var = jnp.mean(jnp.square(y - mu), axis=-1, keepdims=True)
    y = ((y - mu) * lax.rsqrt(var + RW_LN_EPS)).reshape(B, T, D) * W['rw_ln_w'][j] + W['rw_ln_b'][j]
    bonus = jnp.sum(r * k * W['rw_r_k'][j], axis=-1, keepdims=True) * v
    out = ((y + bonus.reshape(B, T, D)) * g).astype(x.dtype) @ W['rw_w_o'][j]
    return out, s_new, x[:, -1]


def memory_kv(mem, w_kv):
    B, M, _ = mem.shape
    kv = jnp.einsum('bmd,lde->lbme', mem, w_kv)
    k, v = jnp.split(kv, 2, axis=-1)
    shape = (w_kv.shape[0], B, M, MEM_HEADS, MEM_HEAD_DIM)
    return k.reshape(shape), v.reshape(shape)


def memory_attn(x, mem_k, mem_v, W, i):
    B, T, _ = x.shape
    q = (x @ W['ca_w_q'][i]).reshape(B, T, MEM_HEADS, MEM_HEAD_DIM)
    s = jnp.einsum('bthd,bmhd->bhtm', q, mem_k.astype(q.dtype)).astype(jnp.float32) * (MEM_HEAD_DIM ** -0.5)
    p = jax.nn.softmax(s, axis=-1).astype(x.dtype)
    o = jnp.einsum('bhtm,bmhd->bthd', p, mem_v.astype(x.dtype)).reshape(B, T, MEM_WIDTH)
    return o @ W['ca_w_o'][i]


def conv_ffn(x, conv0, W, i):
    T = x.shape[1]
    u = x @ W['ffn_w_up'][i]
    up = jnp.concatenate([conv0.astype(u.dtype), u], axis=1)
    cw = W['ffn_conv_w'][i]
    c = sum((cw[t] * up[:, t:t + T] for t in range(CONV_WIDTH)), W['ffn_conv_b'][i])
    val, gate = jnp.split(c, 2, axis=-1)
    return (jax.nn.silu(gate) * val) @ W['ffn_w_down'][i], up[:, T:]


def run_trunk(x, mem_k, mem_v, sb_past_k, sb_past_v, gla_s0, rw_s0, rw_shift0, conv0, W):
    new_k, new_v, new_gla, new_rw, new_shift, new_conv = [], [], [], [], [], []
    for i in range(DEPTH):
        j = i // 2
        if i % 2 == 0:
            h, k_rows, v_rows, s = even_mixer(
                x, None if sb_past_k is None else sb_past_k[j], None if sb_past_v is None else sb_past_v[j],
                gla_s0[j], W, j)
            new_k.append(k_rows)
            new_v.append(v_rows)
            new_gla.append(s)
        else:
            h, s, last = rwkv7_mixer(x, rw_s0[j], rw_shift0[j], W, j)
            new_rw.append(s)
            new_shift.append(last)
        x = layer_norm(DEEPNORM_ALPHA * x + h, W['ln_w'][i, 0], W['ln_b'][i, 0])
        x = layer_norm(DEEPNORM_ALPHA * x + memory_attn(x, mem_k[i], mem_v[i], W, i), W['ln_w'][i, 1], W['ln_b'][i, 1])
        h, c = conv_ffn(x, conv0[i], W, i)
        new_conv.append(c)
        x = layer_norm(DEEPNORM_ALPHA * x + h, W['ln_w'][i, 2], W['ln_b'][i, 2])
    return (x, jnp.stack(new_k), jnp.stack(new_v), jnp.stack(new_gla), jnp.stack(new_rw),
            jnp.stack(new_shift), jnp.stack(new_conv))


def setup_inputs(seed: int = 0) -> dict:
    key = jax.random.key(seed)
    keys = jax.random.split(key, 64)
    counter = iter(range(64))
    nrm = lambda shape, scale=1.0: jax.random.normal(keys[next(counter)], shape, jnp.float32) * scale
    uni = lambda shape, lo, hi: jax.random.uniform(keys[next(counter)], shape, jnp.float32, lo, hi)
    n_pages = PAST_LEN // PAGE_SIZE
    n_used = DEC_BATCH * n_pages
    n_phys = n_used + max(1, n_used // 4)
    perm = jax.random.permutation(keys[next(counter)], n_phys)
    page_table = perm[:n_used].reshape(DEC_BATCH, n_pages).astype(jnp.int32)
    F2 = 2 * FFN_HIDDEN
    return {
        'x_prompt': nrm((BATCH, SEQ, D_MODEL)),
        'x_sample': nrm((DEC_BATCH, DEC_SEQ, D_MODEL)),
        'mem_prompt': nrm((BATCH, N_MEM, D_MODEL)),
        'cache_sb_k': nrm((N_EVEN, n_phys, PAGE_SIZE, SB_HEADS, SB_HEAD_DIM)),
        'cache_sb_v': nrm((N_EVEN, n_phys, PAGE_SIZE, SB_HEADS, SB_HEAD_DIM)),
        'page_table': page_table,
        'state_gla': nrm((N_EVEN, DEC_BATCH, GLA_HEADS, GLA_DK, GLA_DV), 0.5),
        'state_rwkv': nrm((N_ODD, DEC_BATCH, RW_HEADS, RW_HEAD, RW_HEAD), 0.5),
        'state_rwkv_shift': nrm((N_ODD, DEC_BATCH, D_MODEL)),
        'state_ffn_conv': nrm((DEPTH, DEC_BATCH, CONV_WIDTH - 1, F2)),
        'cache_mem_k': nrm((DEPTH, DEC_BATCH, N_MEM, MEM_HEADS, MEM_HEAD_DIM)),
        'cache_mem_v': nrm((DEPTH, DEC_BATCH, N_MEM, MEM_HEADS, MEM_HEAD_DIM)),
        'mx_w_in': nrm((N_EVEN, D_MODEL, EVEN_IN_WIDTH), D_MODEL ** -0.5),
        'sb_bias': SB_BIAS_INIT + nrm((N_EVEN, SB_HEADS), 0.3),
        'gla_w_a2': nrm((N_EVEN, GLA_GATE_RANK, GLA_K_WIDTH), GLA_GATE_RANK ** -0.5),
        'gla_b_a': nrm((N_EVEN, GLA_K_WIDTH), 0.1),
        'gla_norm_w': 1.0 + nrm((N_EVEN, GLA_DV), 0.02),
        'mx_w_out': nrm((N_EVEN, EVEN_OUT_WIDTH, D_MODEL), EVEN_OUT_WIDTH ** -0.5 * DEEPNORM_BETA),
        'rw_mix': uni((N_ODD, 6, D_MODEL), 0.0, 1.0),
        'rw_w_rkv': nrm((N_ODD, 3, D_MODEL, D_MODEL), D_MODEL ** -0.5),
        'rw_w0': uni((N_ODD, D_MODEL), -6.5, -1.5),
        'rw_w1': nrm((N_ODD, D_MODEL, RW_DECAY_RANK), D_MODEL ** -0.5),
        'rw_w2': nrm((N_ODD, RW_DECAY_RANK, D_MODEL), 0.1 * RW_DECAY_RANK ** -0.5),
        'rw_a0': nrm((N_ODD, D_MODEL), 0.1),
        'rw_a1': nrm((N_ODD, D_MODEL, RW_A_RANK), D_MODEL ** -0.5),
        'rw_a2': nrm((N_ODD, RW_A_RANK, D_MODEL), RW_A_RANK ** -0.5),
        'rw_g1': nrm((N_ODD, D_MODEL, RW_GATE_RANK), D_MODEL ** -0.5),
        'rw_g2': nrm((N_ODD, RW_GATE_RANK, D_MODEL), RW_GATE_RANK ** -0.5),
        'rw_k_k': 0.85 + nrm((N_ODD, D_MODEL), 0.02),
        'rw_k_a': 1.0 + nrm((N_ODD, D_MODEL), 0.02),
        'rw_r_k': nrm((N_ODD, RW_HEADS, RW_HEAD), 0.1),
        'rw_ln_w': 1.0 + nrm((N_ODD, D_MODEL), 0.02),
        'rw_ln_b': nrm((N_ODD, D_MODEL), 0.02),
        'rw_w_o': nrm((N_ODD, D_MODEL, D_MODEL), D_MODEL ** -0.5 * DEEPNORM_BETA),
        'ca_w_q': nrm((DEPTH, D_MODEL, MEM_WIDTH), D_MODEL ** -0.5),
        'ca_w_kv': nrm((DEPTH, D_MODEL, 2 * MEM_WIDTH), D_MODEL ** -0.5),
        'ca_w_o': nrm((DEPTH, MEM_WIDTH, D_MODEL), MEM_WIDTH ** -0.5 * DEEPNORM_BETA),
        'ffn_w_up': nrm((DEPTH, D_MODEL, F2), D_MODEL ** -0.5),
        'ffn_conv_w': nrm((DEPTH, CONV_WIDTH, F2), CONV_WIDTH ** -0.5),
        'ffn_conv_b': nrm((DEPTH, F2), 0.02),
        'ffn_w_down': nrm((DEPTH, FFN_HIDDEN, D_MODEL), FFN_HIDDEN ** -0.5 * DEEPNORM_BETA),
        'ln_w': 1.0 + nrm((DEPTH, 3, D_MODEL), 0.02),
        'ln_b': nrm((DEPTH, 3, D_MODEL), 0.02),
    }


def reference(x_prompt, x_sample, mem_prompt, cache_sb_k, cache_sb_v, page_table, state_gla, state_rwkv,
              state_rwkv_shift, state_ffn_conv, cache_mem_k, cache_mem_v, mx_w_in, sb_bias, gla_w_a2, gla_b_a,
              gla_norm_w, mx_w_out, rw_mix, rw_w_rkv, rw_w0, rw_w1, rw_w2, rw_a0, rw_a1, rw_a2, rw_g1, rw_g2,
              rw_k_k, rw_k_a, rw_r_k, rw_ln_w, rw_ln_b, rw_w_o, ca_w_q, ca_w_kv, ca_w_o, ffn_w_up, ffn_conv_w,
              ffn_conv_b, ffn_w_down, ln_w, ln_b):
    W = dict(mx_w_in=mx_w_in, sb_bias=sb_bias, gla_w_a2=gla_w_a2, gla_b_a=gla_b_a, gla_norm_w=gla_norm_w,
             mx_w_out=mx_w_out, rw_mix=rw_mix, rw_w_rkv=rw_w_rkv, rw_w0=rw_w0, rw_w1=rw_w1, rw_w2=rw_w2,
             rw_a0=rw_a0, rw_a1=rw_a1, rw_a2=rw_a2, rw_g1=rw_g1, rw_g2=rw_g2, rw_k_k=rw_k_k, rw_k_a=rw_k_a,
             rw_r_k=rw_r_k, rw_ln_w=rw_ln_w, rw_ln_b=rw_ln_b, rw_w_o=rw_w_o, ca_w_q=ca_w_q, ca_w_kv=ca_w_kv,
             ca_w_o=ca_w_o, ffn_w_up=ffn_w_up, ffn_conv_w=ffn_conv_w, ffn_conv_b=ffn_conv_b,
             ffn_w_down=ffn_w_down, ln_w=ln_w, ln_b=ln_b)
    B = x_prompt.shape[0]
    mem_k_p, mem_v_p = memory_kv(mem_prompt, ca_w_kv)
    gla0 = jnp.zeros((N_EVEN, B, GLA_HEADS, GLA_DK, GLA_DV), jnp.float32)
    rw0 = jnp.zeros((N_ODD, B, RW_HEADS, RW_HEAD, RW_HEAD), jnp.float32)
    sh0 = jnp.zeros((N_ODD, B, D_MODEL), x_prompt.dtype)
    cv0 = jnp.zeros((DEPTH, B, CONV_WIDTH - 1, 2 * FFN_HIDDEN), x_prompt.dtype)
    y_prompt, sbk_p, sbv_p, gla_p, rw_p, sh_p, conv_p = run_trunk(
        x_prompt, mem_k_p, mem_v_p, None, None, gla0, rw0, sh0, cv0, W)
    DB = x_sample.shape[0]
    n_even = cache_sb_k.shape[0]
    past_k = cache_sb_k[:, page_table].reshape(n_even, DB, -1, SB_HEADS, SB_HEAD_DIM)
    past_v = cache_sb_v[:, page_table].reshape(n_even, DB, -1, SB_HEADS, SB_HEAD_DIM)
    y_sample, sbk_s, sbv_s, gla_s, rw_s, sh_s, conv_s = run_trunk(
        x_sample, cache_mem_k, cache_mem_v, past_k, past_v, state_gla, state_rwkv, state_rwkv_shift,
        state_ffn_conv, W)
    return (y_prompt, y_sample, sbk_p, sbv_p, gla_p, rw_p, sh_p, conv_p, mem_k_p, mem_v_p,
            sbk_s, sbv_s, gla_s, rw_s, sh_s, conv_s)
```

```python
import functools

import jax
import jax.numpy as jnp
from jax import lax
from jax.experimental import pallas as pl
from jax.experimental.pallas import tpu as pltpu

F32 = jnp.float32
BF16 = jnp.bfloat16

D_MODEL = 2048
DEPTH = 2
PAGE_SIZE = 128
SB_HEADS = 8
SB_HEAD_DIM = 128
SB_WIDTH = SB_HEADS * SB_HEAD_DIM
GLA_HEADS = 4
GLA_DK = 128
GLA_DV = 256
GLA_K_WIDTH = GLA_HEADS * GLA_DK
GLA_V_WIDTH = GLA_HEADS * GLA_DV
GLA_GATE_RANK = 16
GLA_GATE_PAD = 128
GLA_GATE_NORMALIZER = 16.0
GLA_CHUNK = 64
GLA_NORM_EPS = 1e-5
EVEN_MAIN_WIDTH = 3 * SB_WIDTH + 2 * GLA_K_WIDTH + 2 * GLA_V_WIDTH
RW_HEAD = 64
RW_HEADS = D_MODEL // RW_HEAD
RW_LN_EPS = 64e-5
RW_RANK_PAD = 128
N_MEM = 256
MEM_HEADS = 4
MEM_HEAD_DIM = 128
MEM_WIDTH = MEM_HEADS * MEM_HEAD_DIM
FFN_HIDDEN = 5504
FFN_PAD = 5632
LN_EPS = 1e-5
ALPHA = (2.0 * DEPTH) ** 0.25

LANES = 128
SUBLANES = 8
VMEM_LIMIT_MB = 56

NT_DIMS = (((1,), (1,)), ((), ()))
TN_DIMS = (((0,), (0,)), ((), ()))


def _row_dtype(rows):
    return BF16 if rows % (2 * SUBLANES) == 0 else F32


def _cp(*sem):
    return pltpu.CompilerParams(dimension_semantics=sem, vmem_limit_bytes=VMEM_LIMIT_MB * 1024 * 1024)


def _dot(a, b):
    return jnp.dot(a, b, preferred_element_type=F32)


def _dot_nt(a, b):
    return lax.dot_general(a, b, NT_DIMS, preferred_element_type=F32)


def _softplus_neg_abs(z):
    return jnp.log1p(jnp.exp(-jnp.abs(z)))


def _log_sigmoid(z):
    return jnp.minimum(z, 0.0) - _softplus_neg_abs(z)


def _sigmoid(z):
    return 1.0 / (1.0 + jnp.exp(-z))


def _split_bf16(x):
    hi = x.astype(BF16)
    lo = (x - hi.astype(F32)).astype(BF16)
    return hi, lo


def _layer_norm(y, w, b):
    mu = jnp.mean(y, axis=-1, keepdims=True)
    d = y - mu
    var = jnp.mean(d * d, axis=-1, keepdims=True)
    return d * lax.rsqrt(var + LN_EPS) * w + b


def _mm_kernel(x_ref, w_ref, o_ref):
    o_ref[...] = _dot(x_ref[...].astype(BF16), w_ref[...]).astype(o_ref.dtype)


def _mm(x, w, n_out, *, tm, tn, out_dtype=F32):
    m, k = x.shape
    return pl.pallas_call(
        _mm_kernel,
        grid=(m // tm, n_out // tn),
        in_specs=[pl.BlockSpec((tm, k), lambda i, j: (i, 0)),
                  pl.BlockSpec((k, tn), lambda i, j: (0, j))],
        out_specs=pl.BlockSpec((tm, tn), lambda i, j: (i, j)),
        out_shape=jax.ShapeDtypeStruct((m, n_out), out_dtype),
        compiler_params=_cp("parallel", "arbitrary"),
        name="mm",
    )(x, w)


def _mm_ln_kernel(*refs, nk, gated):
    if gated:
        x_ref, g_ref, w_ref, res_ref, lw_ref, lb_ref, o_ref = refs[:7]
        x = (x_ref[...] * g_ref[...]).astype(BF16)
    else:
        x_ref, w_ref, res_ref, lw_ref, lb_ref, o_ref = refs[:6]
        x = x_ref[...].astype(BF16)
    part = _dot(x, w_ref[...])

    def finish(h):
        o_ref[...] = _layer_norm(ALPHA * res_ref[...] + h, lw_ref[...], lb_ref[...])

    if nk == 1:
        finish(part)
        return
    acc_ref = refs[-1]
    kk = pl.program_id(1)

    @pl.when(kk == 0)
    def _():
        acc_ref[...] = part

    @pl.when(kk > 0)
    def _():
        acc_ref[...] += part

    @pl.when(kk == nk - 1)
    def _():
        finish(acc_ref[...])


def _mm_ln(x, w, res, lw, lb, *, tm, tk, gate=None):
    m, k = x.shape
    n = w.shape[1]
    nk = k // tk
    gated = gate is not None
    xs = pl.BlockSpec((tm, tk), lambda i, kk: (i, kk))
    row = pl.BlockSpec((tm, n), lambda i, kk: (i, 0))
    vec = pl.BlockSpec((1, n), lambda i, kk: (0, 0))
    in_specs = [xs] + ([xs] if gated else []) + [pl.BlockSpec((tk, n), lambda i, kk: (kk, 0)), row, vec, vec]
    args = [x] + ([gate] if gated else []) + [w, res, lw.reshape(1, n), lb.reshape(1, n)]
    return pl.pallas_call(
        functools.partial(_mm_ln_kernel, nk=nk, gated=gated),
        grid=(m // tm, nk),
        in_specs=in_specs,
        out_specs=row,
        out_shape=jax.ShapeDtypeStruct((m, n), F32),
        scratch_shapes=[pltpu.VMEM((tm, n), F32)] if nk > 1 else [],
        compiler_params=_cp("parallel", "arbitrary"),
        name="mm_ln",
    )(*args)


def _ca_kernel(x_ref, wq_ref, mk_ref, mv_ref, wo_ref, lw_ref, lb_ref, o_ref, ob_ref):
    x = x_ref[...]
    q = _dot(x.astype(BF16), wq_ref[...])
    heads = []
    for h in range(MEM_HEADS):
        cols = slice(h * MEM_HEAD_DIM, (h + 1) * MEM_HEAD_DIM)
        s = _dot_nt(q[:, cols].astype(BF16), mk_ref[:, cols].astype(BF16)) * (MEM_HEAD_DIM ** -0.5)
        e = jnp.exp(s - jnp.max(s, axis=-1, keepdims=True))
        p = e / jnp.sum(e, axis=-1, keepdims=True)
        heads.append(_dot(p.astype(BF16), mv_ref[:, cols].astype(BF16)))
    o = jnp.concatenate(heads, axis=-1)
    y = _layer_norm(ALPHA * x + _dot(o.astype(BF16), wo_ref[...]), lw_ref[...], lb_ref[...])
    o_ref[...] = y
    ob_ref[...] = y.astype(ob_ref.dtype)


def _cross_attn(x, mem_k, mem_v, k_col, v_col, wq, wo, lw, lb, *, batch, tm):
    m = x.shape[0]
    tiles = m // batch // tm
    vec = pl.BlockSpec((1, D_MODEL), lambda b, i: (0, 0))
    row = pl.BlockSpec((tm, D_MODEL), lambda b, i: (b * tiles + i, 0))
    return pl.pallas_call(
        _ca_kernel,
        grid=(batch, tiles),
        in_specs=[row,
                  pl.BlockSpec((D_MODEL, MEM_WIDTH), lambda b, i: (0, 0)),
                  pl.BlockSpec((None, N_MEM, MEM_WIDTH), lambda b, i: (b, 0, k_col)),
                  pl.BlockSpec((None, N_MEM, MEM_WIDTH), lambda b, i: (b, 0, v_col)),
                  pl.BlockSpec((MEM_WIDTH, D_MODEL), lambda b, i: (0, 0)),
                  vec, vec],
        out_specs=[row, row],
        out_shape=[jax.ShapeDtypeStruct((m, D_MODEL), F32), jax.ShapeDtypeStruct((m, D_MODEL), _row_dtype(tm))],
        compiler_params=_cp("parallel", "arbitrary"),
        name="cross_attn",
    )(x, wq, mem_k, mem_v, wo, lw.reshape(1, -1), lb.reshape(1, -1))


def _ffn_up_kernel(x_ref, wv_ref, wg_ref, cwv_ref, cwg_ref, cbv_ref, cbg_ref, c0v_ref, c0g_ref,
                   act_ref, csv_ref, csg_ref, bufv, bufg, *, tm, shift, carry_rows, tiles_per_seq):
    i = pl.program_id(1)
    cr = carry_rows

    @pl.when(i % tiles_per_seq == 0)
    def _():
        bufv[0:cr, :] = c0v_ref[...]
        bufg[0:cr, :] = c0g_ref[...]

    x = x_ref[...].astype(BF16)
    bufv[cr:cr + tm, :] = _dot(x, wv_ref[...])
    bufg[cr:cr + tm, :] = _dot(x, wg_ref[...])

    def conv(buf, cw_ref, cb_ref):
        return (cw_ref[0:1, :] * buf[cr - 2 * shift:cr - 2 * shift + tm, :]
                + cw_ref[1:2, :] * buf[cr - shift:cr - shift + tm, :]
                + cw_ref[2:3, :] * buf[cr:cr + tm, :] + cb_ref[...])

    val = conv(bufv, cwv_ref, cbv_ref)
    gate = conv(bufg, cwg_ref, cbg_ref)
    act_ref[...] = (gate * _sigmoid(gate) * val).astype(act_ref.dtype)
    lastv = bufv[tm:tm + cr, :]
    lastg = bufg[tm:tm + cr, :]
    csv_ref[...] = lastv
    csg_ref[...] = lastg
    bufv[0:cr, :] = lastv
    bufg[0:cr, :] = lastg


def _ffn_up(xb, wv, wg, cwv, cwg, cbv, cbg, c0v, c0g, *, tm, tn, shift, seqs):
    m, k = xb.shape
    carry_rows = c0v.shape[1]
    tiles_per_seq = m // seqs // tm
    col = lambda j, i: (0, j)
    st = pl.BlockSpec((None, carry_rows, tn), lambda j, i: (i // tiles_per_seq, 0, j))
    return pl.pallas_call(
        functools.partial(_ffn_up_kernel, tm=tm, shift=shift, carry_rows=carry_rows, tiles_per_seq=tiles_per_seq),
        grid=(FFN_PAD // tn, m // tm),
        in_specs=[pl.BlockSpec((tm, k), lambda j, i: (i, 0)),
                  pl.BlockSpec((k, tn), col), pl.BlockSpec((k, tn), col),
                  pl.BlockSpec((3, tn), col), pl.BlockSpec((3, tn), col),
                  pl.BlockSpec((1, tn), col), pl.BlockSpec((1, tn), col),
                  st, st],
        out_specs=[pl.BlockSpec((tm, tn), lambda j, i: (i, j)), st, st],
        out_shape=[jax.ShapeDtypeStruct((m, FFN_PAD), BF16),
                   jax.ShapeDtypeStruct((seqs, carry_rows, FFN_PAD), F32),
                   jax.ShapeDtypeStruct((seqs, carry_rows, FFN_PAD), F32)],
        scratch_shapes=[pltpu.VMEM((tm + carry_rows, tn), F32), pltpu.VMEM((tm + carry_rows, tn), F32)],
        compiler_params=_cp("parallel", "arbitrary"),
        name="ffn_up",
    )(xb, wv, wg, cwv, cwg, cbv, cbg, c0v, c0g)


def _sb_weights(z, carry, tri, mask, key_axis):
    sp = _softplus_neg_abs(z)
    log_beta = jnp.minimum(z, 0.0) - sp
    log_keep = jnp.minimum(-z, 0.0) - sp
    if mask is not None:
        log_keep = jnp.where(mask, log_keep, 0.0)
    hi, lo = _split_bf16(log_keep)
    if key_axis == 1:
        tail = _dot(hi, tri) + _dot(lo, tri)
    else:
        tail = _dot(tri, hi) + _dot(tri, lo)
    a = jnp.exp(log_beta + tail + carry)
    if mask is not None:
        a = jnp.where(mask, a, 0.0)
    return a, carry + jnp.sum(log_keep, axis=key_axis, keepdims=True)


def _sb_prompt_kernel(bias_ref, q_ref, k_ref, v_ref, o_ref, *, tq):
    h = pl.program_id(1)
    qi = pl.program_id(2)
    bias = bias_ref[h]
    tk = LANES
    nsub = tq // tk
    q = q_ref[...].astype(BF16)
    r = lax.broadcasted_iota(jnp.int32, (tk, tk), 0)
    c = lax.broadcasted_iota(jnp.int32, (tk, tk), 1)
    tri = jnp.where(r > c, 1.0, 0.0).astype(BF16)
    trow = lax.broadcasted_iota(jnp.int32, (tq, tk), 0)
    scol = lax.broadcasted_iota(jnp.int32, (tq, tk), 1)

    def block(start, acc, carry, mask):
        k = k_ref[pl.ds(start, tk), :].astype(BF16)
        v = v_ref[pl.ds(start, tk), :].astype(BF16)
        z = _dot_nt(q, k) * (SB_HEAD_DIM ** -0.5) + bias
        a, carry = _sb_weights(z, carry, tri, mask, 1)
        return acc + _dot(a.astype(BF16), v), carry

    acc = jnp.zeros((tq, SB_HEAD_DIM), F32)
    carry = jnp.zeros((tq, 1), F32)
    for d in reversed(range(nsub)):
        start = pl.multiple_of(qi * tq + d * tk, tk)
        acc, carry = block(start, acc, carry, (scol + d * tk) < trow)

    def body(it, state):
        kb = qi * nsub - 1 - it
        return block(pl.multiple_of(kb * tk, tk), state[0], state[1], None)

    acc, carry = lax.fori_loop(0, qi * nsub, body, (acc, carry))
    o_ref[...] = acc.astype(o_ref.dtype)


def _sb_prompt(proj, bias, *, batch, seq, tq):
    nq = seq // tq
    grid_spec = pltpu.PrefetchScalarGridSpec(
        num_scalar_prefetch=1,
        grid=(batch, SB_HEADS, nq),
        in_specs=[pl.BlockSpec((tq, SB_HEAD_DIM), lambda b, h, i, s: (b * nq + i, h)),
                  pl.BlockSpec((seq, SB_HEAD_DIM), lambda b, h, i, s: (b, SB_HEADS + h)),
                  pl.BlockSpec((seq, SB_HEAD_DIM), lambda b, h, i, s: (b, 2 * SB_HEADS + h))],
        out_specs=pl.BlockSpec((tq, SB_HEAD_DIM), lambda b, h, i, s: (b * nq + i, h)),
    )
    return pl.pallas_call(
        functools.partial(_sb_prompt_kernel, tq=tq),
        grid_spec=grid_spec,
        out_shape=jax.ShapeDtypeStruct((batch * seq, SB_WIDTH), BF16),
        compiler_params=_cp("parallel", "parallel", "arbitrary"),
        name="sb_prompt",
    )(bias, proj, proj, proj)


def _sb_sample_kernel(pt_ref, q_ref, bias_ref, kn_ref, vn_ref, kc_ref, vc_ref, o_ref, acc_ref, carry_ref):
    s = pl.program_id(1)
    n = LANES
    q = q_ref[...]
    row = lax.broadcasted_iota(jnp.int32, (n, n), 0)
    col = lax.broadcasted_iota(jnp.int32, (n, n), 1)
    tri = jnp.where(col > row, 1.0, 0.0).astype(BF16)
    col_head = col // SUBLANES
    col_t = col % SUBLANES

    def page(k_ref, v_ref, mask):
        z = jnp.zeros((n, n), F32)
        for h in range(SB_HEADS):
            kh = k_ref[pl.ds(h, PAGE_SIZE, stride=SB_HEADS), :].astype(BF16)
            z = jnp.where(col_head == h, _dot_nt(kh, q), z)
        z = z * (SB_HEAD_DIM ** -0.5) + bias_ref[...]
        a, carry = _sb_weights(z, carry_ref[...], tri, mask, 0)
        carry_ref[...] = carry
        at = a.T.astype(BF16)
        for h in range(SB_HEADS):
            rows = slice(h * SUBLANES, (h + 1) * SUBLANES)
            vh = v_ref[pl.ds(h, PAGE_SIZE, stride=SB_HEADS), :].astype(BF16)
            acc_ref[rows, :] += _dot(at[rows, :], vh)

    @pl.when(s == 0)
    def _():
        acc_ref[...] = jnp.zeros_like(acc_ref)
        carry_ref[...] = jnp.zeros_like(carry_ref)
        page(kn_ref, vn_ref, row < col_t)

    page(kc_ref, vc_ref, None)

    @pl.when(s == pl.num_programs(1) - 1)
    def _():
        o_ref[...] = acc_ref[...]


def _sb_sample(q_rows, bias_cols, k_new, v_new, cache_k, cache_v, page_table):
    nb, n_pages = page_table.shape
    rows = PAGE_SIZE * SB_HEADS

    def page_idx(b, s, pt):
        return (pt[b, n_pages - 1 - s], 0, 0)

    per_b = lambda b, s, pt: (b, 0, 0)
    grid_spec = pltpu.PrefetchScalarGridSpec(
        num_scalar_prefetch=1,
        grid=(nb, n_pages),
        in_specs=[pl.BlockSpec((None, LANES, SB_HEAD_DIM), per_b),
                  pl.BlockSpec((1, LANES), lambda b, s, pt: (0, 0)),
                  pl.BlockSpec((None, rows, SB_HEAD_DIM), per_b),
                  pl.BlockSpec((None, rows, SB_HEAD_DIM), per_b),
                  pl.BlockSpec((None, rows, SB_HEAD_DIM), page_idx),
                  pl.BlockSpec((None, rows, SB_HEAD_DIM), page_idx)],
        out_specs=pl.BlockSpec((None, LANES, SB_HEAD_DIM), per_b),
        scratch_shapes=[pltpu.VMEM((LANES, SB_HEAD_DIM), F32), pltpu.VMEM((1, LANES), F32)],
    )
    return pl.pallas_call(
        _sb_sample_kernel,
        grid_spec=grid_spec,
        out_shape=jax.ShapeDtypeStruct((nb, LANES, SB_HEAD_DIM), F32),
        compiler_params=_cp("parallel", "arbitrary"),
        name="sb_sample",
    )(page_table, q_rows, bias_cols, k_new, v_new, cache_k, cache_v)


def _gla_kernel(q_ref, k_ref, v_ref, g_ref, low_ref, wa_ref, ba_ref, nw_ref, s0_ref, o_ref, sout_ref, st_ref,
                *, rows, nchunks):
    ci = pl.program_id(2)
    c = GLA_CHUNK

    @pl.when(ci == 0)
    def _():
        st_ref[...] = s0_ref[...]

    def pad(x):
        if rows == c:
            return x
        return jnp.concatenate([x, jnp.zeros((c - rows, x.shape[1]), x.dtype)], axis=0)

    q = pad(q_ref[...]) * (GLA_DK ** -0.5)
    k = pad(k_ref[...])
    v = pad(v_ref[...])
    low = pad(low_ref[...])
    r = lax.broadcasted_iota(jnp.int32, (c, c), 0)
    cc = lax.broadcasted_iota(jnp.int32, (c, c), 1)
    causal = r >= cc
    log_a = _log_sigmoid(_dot(low.astype(BF16), wa_ref[...]) + ba_ref[...]) * (1.0 / GLA_GATE_NORMALIZER)
    if rows < c:
        log_a = jnp.where(lax.broadcasted_iota(jnp.int32, (c, GLA_DK), 0) < rows, log_a, 0.0)
    hi, lo = _split_bf16(log_a)
    ltri = jnp.where(causal, 1.0, 0.0).astype(BF16)
    b = _dot(ltri, hi) + _dot(ltri, lo)
    b_last = b[c - 1:c, :]
    q_dec = (q * jnp.exp(b)).astype(BF16)
    scores = _dot_nt(q_dec, (k * jnp.exp(-b)).astype(BF16))
    o = _dot(jnp.where(causal, scores, 0.0).astype(BF16), v.astype(BF16))
    st = st_ref[...]
    o = o + _dot_nt(q_dec, st.astype(BF16))
    k_state = (k * jnp.exp(b_last - b)).astype(BF16)
    st_ref[...] = st * jnp.exp(b_last) + lax.dot_general(v.astype(BF16), k_state, TN_DIMS,
                                                         preferred_element_type=F32)
    o = o * lax.rsqrt(jnp.mean(o * o, axis=-1, keepdims=True) + GLA_NORM_EPS) * nw_ref[...]
    g = pad(g_ref[...])
    o = o * (g * _sigmoid(g))
    o_ref[...] = o[:rows, :].astype(o_ref.dtype)

    @pl.when(ci == nchunks - 1)
    def _():
        sout_ref[...] = st_ref[...]


def _gla(proj, low, wa, ba, nw, s0t, *, batch, seq):
    rows = GLA_CHUNK if seq % GLA_CHUNK == 0 else seq
    nchunks = seq // rows
    qc = 3 * SB_WIDTH // GLA_DK
    kc = qc + GLA_HEADS
    vc = (3 * SB_WIDTH + 2 * GLA_K_WIDTH) // GLA_DV
    gc = vc + GLA_HEADS
    rowblk = lambda b, h, i: b * nchunks + i
    st = pl.BlockSpec((None, None, GLA_DV, GLA_DK), lambda b, h, i: (b, h, 0, 0))
    return pl.pallas_call(
        functools.partial(_gla_kernel, rows=rows, nchunks=nchunks),
        grid=(batch, GLA_HEADS, nchunks),
        in_specs=[pl.BlockSpec((rows, GLA_DK), lambda b, h, i: (rowblk(b, h, i), qc + h)),
                  pl.BlockSpec((rows, GLA_DK), lambda b, h, i: (rowblk(b, h, i), kc + h)),
                  pl.BlockSpec((rows, GLA_DV), lambda b, h, i: (rowblk(b, h, i), vc + h)),
                  pl.BlockSpec((rows, GLA_DV), lambda b, h, i: (rowblk(b, h, i), gc + h)),
                  pl.BlockSpec((rows, GLA_GATE_PAD), lambda b, h, i: (rowblk(b, h, i), 0)),
                  pl.BlockSpec((GLA_GATE_PAD, GLA_DK), lambda b, h, i: (0, h)),
                  pl.BlockSpec((1, GLA_DK), lambda b, h, i: (0, h)),
                  pl.BlockSpec((1, GLA_DV), lambda b, h, i: (0, 0)),
                  st],
        out_specs=[pl.BlockSpec((rows, GLA_DV), lambda b, h, i: (rowblk(b, h, i), h)), st],
        out_shape=[jax.ShapeDtypeStruct((batch * seq, GLA_V_WIDTH), _row_dtype(rows)),
                   jax.ShapeDtypeStruct((batch, GLA_HEADS, GLA_DV, GLA_DK), F32)],
        scratch_shapes=[pltpu.VMEM((GLA_DV, GLA_DK), F32)],
        compiler_params=_cp("parallel", "parallel", "arbitrary"),
        name="gla",
    )(proj, proj, proj, proj, low, wa, ba.reshape(1, -1), nw.reshape(1, -1), s0t)


def _rw_mix_kernel(x_ref, xp_ref, mix_ref, *o_refs):
    x = x_ref[...]
    xx = xp_ref[...] - x
    for m, o_ref in enumerate(o_refs):
        o_ref[...] = (x + xx * mix_ref[m:m + 1, :]).astype(o_ref.dtype)


def _rw_mix(x, x_prev, mix, *, tm):
    m, d = x.shape
    row = pl.BlockSpec((tm, d), lambda i: (i, 0))
    return pl.pallas_call(
        _rw_mix_kernel,
        grid=(m // tm,),
        in_specs=[row, row, pl.BlockSpec((6, d), lambda i: (0, 0))],
        out_specs=[row] * 6,
        out_shape=[jax.ShapeDtypeStruct((m, d), BF16)] * 6,
        compiler_params=_cp("parallel"),
        name="rw_mix",
    )(x, x_prev, mix)


def _rw_lora_kernel(xw_ref, xa_ref, xg_ref, w1_ref, w2_ref, a1_ref, a2_ref, g1_ref, g2_ref, w0_ref, a0_ref,
                    dec_ref, a_ref, g_ref):
    hw = jnp.tanh(_dot(xw_ref[...], w1_ref[...]))
    wl = w0_ref[...] + _dot(hw.astype(BF16), w2_ref[...])
    w_log = _log_sigmoid(wl) - 0.5
    dec_ref[...] = jnp.exp(-jnp.exp(w_log))
    ha = _dot(xa_ref[...], a1_ref[...])
    a_ref[...] = _sigmoid(a0_ref[...] + _dot(ha.astype(BF16), a2_ref[...]))
    hg = _sigmoid(_dot(xg_ref[...], g1_ref[...]))
    g_ref[...] = _dot(hg.astype(BF16), g2_ref[...])


def _rw_lora(xw, xa, xg, w1, w2, a1, a2, g1, g2, w0, a0, *, tm):
    m, d = xw.shape
    row = pl.BlockSpec((tm, d), lambda i: (i, 0))
    full = lambda a: pl.BlockSpec(a.shape, lambda i: (0, 0))
    w0 = w0.reshape(1, d)
    a0 = a0.reshape(1, d)
    consts = [w1, w2, a1, a2, g1, g2, w0, a0]
    return pl.pallas_call(
        _rw_lora_kernel,
        grid=(m // tm,),
        in_specs=[row, row, row] + [full(a) for a in consts],
        out_specs=[row] * 3,
        out_shape=[jax.ShapeDtypeStruct((m, d), F32)] * 3,
        compiler_params=_cp("parallel"),
        name="rw_lora",
    )(xw, xa, xg, *consts)


def _rw_scan_kernel(r_ref, w_ref, k_ref, a_ref, v_ref, kkw_ref, kaw_ref, rkw_ref, lnw_ref, lnb_ref, s0_ref,
                    y_ref, sout_ref, s_ref, kk_ref, bb_ref, km_ref, *, tt, nblk):
    tb = pl.program_id(1)
    n = RW_HEAD

    @pl.when(tb == 0)
    def _():
        s_ref[...] = s0_ref[...]

    k = k_ref[...]
    a = a_ref[...]
    kk = k * kkw_ref[...]
    kk = kk * lax.rsqrt(jnp.maximum(jnp.sum(kk * kk, axis=1, keepdims=True), 1e-24))
    km = k * (1.0 + (a - 1.0) * kaw_ref[...])
    kk_ref[...] = kk
    bb_ref[...] = kk * a
    km_ref[...] = km
    bonus = jnp.sum(r_ref[...] * km * rkw_ref[...], axis=1, keepdims=True)

    sa0 = jnp.zeros((n // 2, LANES), F32)
    for j in range(n):
        sa0 = sa0 - s_ref[j] * kk_ref[0, j:j + 1, :]

    def step(t, sa):
        t_next = jnp.minimum(t + 1, tt - 1)
        vv = v_ref[t]
        y_acc = jnp.zeros((n // 2, LANES), F32)
        sa_acc = jnp.zeros((n // 2, LANES), F32)
        for j in range(n):
            sj = (s_ref[j] * w_ref[t, j:j + 1, :] + sa * bb_ref[t, j:j + 1, :] + vv * km_ref[t, j:j + 1, :])
            s_ref[j] = sj
            y_acc = y_acc + sj * r_ref[t, j:j + 1, :]
            sa_acc = sa_acc - sj * kk_ref[t_next, j:j + 1, :]
        y_ref[t] = y_acc
        return sa_acc

    lax.fori_loop(0, tt, step, sa0)

    y = y_ref[...]

    def head_mean(x):
        part = jnp.sum(x, axis=1, keepdims=True)
        return (part + pltpu.roll(part, LANES // 2, axis=2)) * (1.0 / n)

    d = y - head_mean(y)
    yn = d * lax.rsqrt(head_mean(d * d) + RW_LN_EPS) * lnw_ref[...] + lnb_ref[...]
    y_ref[...] = yn + bonus * v_ref[...]

    @pl.when(tb == nblk - 1)
    def _():
        sout_ref[...] = s_ref[...]


def _rw_scan(r, w, k, a, v, kkw, kaw, rkw, lnw, lnb, s0, *, tt):
    g, t, n, _ = r.shape
    nblk = t // tt
    kblk = pl.BlockSpec((None, tt, n, LANES), lambda gi, ti: (gi, ti, 0, 0))
    vblk = pl.BlockSpec((None, tt, n // 2, LANES), lambda gi, ti: (gi, ti, 0, 0))
    kconst = pl.BlockSpec((n, LANES), lambda gi, ti: (0, 0))
    vconst = pl.BlockSpec((n // 2, LANES), lambda gi, ti: (0, 0))
    sblk = pl.BlockSpec((None, n, n // 2, LANES), lambda gi, ti: (gi, 0, 0, 0))
    return pl.pallas_call(
        functools.partial(_rw_scan_kernel, tt=tt, nblk=nblk),
        grid=(g, nblk),
        in_specs=[kblk, kblk, kblk, kblk, vblk, kconst, kconst, kconst, vconst, vconst, sblk],
        out_specs=[vblk, sblk],
        out_shape=[jax.ShapeDtypeStruct((g, t, n // 2, LANES), F32),
                   jax.ShapeDtypeStruct((g, n, n // 2, LANES), F32)],
        scratch_shapes=[pltpu.VMEM((n, n // 2, LANES), F32)] + [pltpu.VMEM((tt, n, LANES), F32)] * 3,
        compiler_params=_cp("parallel", "arbitrary"),
        name="rw_scan",
    )(r, w, k, a, v, kkw, kaw, rkw, lnw, lnb, s0)


def _to_scan_k(x, g):
    b, t, _ = x.shape
    x = x.reshape(g, 2, t, RW_HEADS, RW_HEAD).transpose(0, 2, 4, 1, 3).reshape(g, t, RW_HEAD, 2 * RW_HEADS)
    return jnp.concatenate([x, x], axis=-1)


def _to_scan_v(x, g):
    b, t, _ = x.shape
    x = x.reshape(g, 2, t, RW_HEADS, 2, RW_HEAD // 2).transpose(0, 2, 5, 4, 1, 3)
    return x.reshape(g, t, RW_HEAD // 2, LANES)


def _from_scan_v(y, g):
    _, t, _, _ = y.shape
    y = y.reshape(g, t, RW_HEAD // 2, 2, 2, RW_HEADS).transpose(0, 4, 1, 5, 3, 2)
    return y.reshape(g * 2, t, D_MODEL)


def _weight_scan_k(w):
    return jnp.tile(w.reshape(RW_HEADS, RW_HEAD).T, (1, 4))


def _weight_scan_v(w):
    w = w.reshape(RW_HEADS, 2, RW_HEAD // 2).transpose(2, 1, 0)[:, :, None, :]
    return jnp.broadcast_to(w, (RW_HEAD // 2, 2, 2, RW_HEADS)).reshape(RW_HEAD // 2, LANES)


def _state_to_scan(s, g):
    s = s.reshape(g, 2, RW_HEADS, 2, RW_HEAD // 2, RW_HEAD).transpose(0, 5, 4, 3, 1, 2)
    return s.reshape(g, RW_HEAD, RW_HEAD // 2, LANES)


def _state_from_scan(s, g):
    s = s.reshape(g, RW_HEAD, RW_HEAD // 2, 2, 2, RW_HEADS).transpose(0, 4, 5, 3, 2, 1)
    return s.reshape(g * 2, RW_HEADS, RW_HEAD, RW_HEAD)


def _prep_weights(p):
    w = {}
    w["mx_w_in"] = [p["mx_w_in"][j].astype(BF16) for j in range(p["mx_w_in"].shape[0])]
    gpad = GLA_GATE_PAD - GLA_GATE_RANK
    w["mx_w_low"] = [jnp.pad(p["mx_w_in"][j][:, EVEN_MAIN_WIDTH:], ((0, 0), (0, gpad))).astype(BF16)
                     for j in range(p["mx_w_in"].shape[0])]
    w["gla_w_a2"] = jnp.pad(p["gla_w_a2"], ((0, 0), (0, gpad), (0, 0))).astype(BF16)
    w["mx_w_out"] = p["mx_w_out"].astype(BF16)
    w["rw_w_rkv"] = p["rw_w_rkv"].astype(BF16)
    pad_c = lambda a: jnp.pad(a, ((0, 0), (0, 0), (0, RW_RANK_PAD - a.shape[2]))).astype(BF16)
    pad_r = lambda a: jnp.pad(a, ((0, 0), (0, RW_RANK_PAD - a.shape[1]), (0, 0))).astype(BF16)
    w["rw_w1"], w["rw_w2"] = pad_c(p["rw_w1"]), pad_r(p["rw_w2"])
    w["rw_a1"], w["rw_a2"] = pad_c(p["rw_a1"]), pad_r(p["rw_a2"])
    w["rw_g1"], w["rw_g2"] = p["rw_g1"].astype(BF16), p["rw_g2"].astype(BF16)
    w["rw_w_o"] = p["rw_w_o"].astype(BF16)
    w["ca_w_q"] = p["ca_w_q"].astype(BF16)
    w["ca_w_kv"] = p["ca_w_kv"].astype(BF16)
    w["ca_w_o"] = p["ca_w_o"].astype(BF16)
    fpad = FFN_PAD - FFN_HIDDEN
    halves = lambda a: (jnp.pad(a[..., :FFN_HIDDEN], ((0, 0),) * (a.ndim - 1) + ((0, fpad),)),
                        jnp.pad(a[..., FFN_HIDDEN:], ((0, 0),) * (a.ndim - 1) + ((0, fpad),)))
    upv, upg = halves(p["ffn_w_up"])
    w["ffn_up_v"], w["ffn_up_g"] = upv.astype(BF16), upg.astype(BF16)
    w["ffn_cw_v"], w["ffn_cw_g"] = halves(p["ffn_conv_w"])
    cbv, cbg = halves(p["ffn_conv_b"])
    w["ffn_cb_v"], w["ffn_cb_g"] = cbv[:, None, :], cbg[:, None, :]
    w["ffn_down"] = jnp.pad(p["ffn_w_down"], ((0, 0), (0, fpad), (0, 0))).astype(BF16)
    for name in ("sb_bias", "gla_b_a", "gla_norm_w", "rw_mix", "rw_w0", "rw_a0", "rw_k_k", "rw_k_a", "rw_r_k",
                 "rw_ln_w", "rw_ln_b", "ln_w", "ln_b"):
        w[name] = p[name]
    return w


def _split_conv_state(c):
    fpad = FFN_PAD - FFN_HIDDEN
    pad = ((0, 0), (0, 0), (0, fpad))
    return jnp.pad(c[..., :FFN_HIDDEN], pad), jnp.pad(c[..., FFN_HIDDEN:], pad)


def _trunk(x, mem_k, mem_v, mem_cols, sb_past, page_table, gla_s0, rw_s0, rw_shift0, conv0, w, *, prompt):
    bsz, seq, _ = x.shape
    m = bsz * seq
    tm = 512 if prompt else m
    x = x.reshape(m, D_MODEL)
    new_k, new_v, new_gla, new_rw, new_shift, new_conv = [], [], [], [], [], []
    for i in range(DEPTH):
        j = i // 2
        lw, lb = w["ln_w"][i], w["ln_b"][i]
        if i % 2 == 0:
            proj = _mm(x, w["mx_w_in"][j], EVEN_MAIN_WIDTH, tm=min(m, 1024), tn=512)
            low = _mm(x, w["mx_w_low"][j], GLA_GATE_PAD, tm=min(m, 1024), tn=GLA_GATE_PAD)
            ka = proj[:, SB_WIDTH:2 * SB_WIDTH].reshape(bsz, seq, SB_HEADS, SB_HEAD_DIM)
            va = proj[:, 2 * SB_WIDTH:3 * SB_WIDTH].reshape(bsz, seq, SB_HEADS, SB_HEAD_DIM)
            new_k.append(ka)
            new_v.append(va)
            if prompt:
                oa = _sb_prompt(proj, w["sb_bias"][j], batch=bsz, seq=seq, tq=256)
            else:
                cache_k, cache_v = sb_past
                n_phys = cache_k.shape[1]
                rows = PAGE_SIZE * SB_HEADS
                qa = proj[:, :SB_WIDTH].reshape(bsz, seq, SB_HEADS, SB_HEAD_DIM).transpose(0, 2, 1, 3)
                qa = qa.reshape(bsz, SB_HEADS * seq, SB_HEAD_DIM).astype(BF16)
                qa = jnp.pad(qa, ((0, 0), (0, LANES - SB_HEADS * seq), (0, 0)))
                bias_cols = jnp.pad(jnp.repeat(w["sb_bias"][j], seq), (0, LANES - SB_HEADS * seq)).reshape(1, LANES)
                pad_new = lambda a: jnp.pad(a.reshape(bsz, seq * SB_HEADS, SB_HEAD_DIM),
                                            ((0, 0), (0, rows - seq * SB_HEADS), (0, 0)))
                oa = _sb_sample(qa, bias_cols, pad_new(ka), pad_new(va),
                                cache_k[j].reshape(n_phys, rows, SB_HEAD_DIM),
                                cache_v[j].reshape(n_phys, rows, SB_HEAD_DIM), page_table)
                oa = oa[:, :SB_HEADS * seq].reshape(bsz, SB_HEADS, seq, SB_HEAD_DIM).transpose(0, 2, 1, 3)
                oa = oa.reshape(m, SB_WIDTH).astype(BF16)
            ob, s_t = _gla(proj, low, w["gla_w_a2"][j], w["gla_b_a"][j], w["gla_norm_w"][j],
                           gla_s0[j].transpose(0, 1, 3, 2), batch=bsz, seq=seq)
            new_gla.append(s_t.transpose(0, 1, 3, 2))
            o = jnp.concatenate([oa, ob.astype(BF16)], axis=-1)
            x = _mm_ln(o, w["mx_w_out"][j], x, lw[0], lb[0], tm=tm, tk=o.shape[1])
        else:
            g = bsz // 2
            x3 = x.reshape(bsz, seq, D_MODEL)
            new_shift.append(x3[:, -1])
            x_prev = jnp.concatenate([rw_shift0[j][:, None, :], x3[:, :-1]], axis=1).reshape(m, D_MODEL)
            xr, xw, xk, xv, xa, xg = _rw_mix(x, x_prev, w["rw_mix"][j], tm=tm)
            r, k, v = (_mm(xi, w["rw_w_rkv"][j][n], D_MODEL, tm=min(m, 1024), tn=512)
                       for n, xi in enumerate((xr, xk, xv)))
            dec, a, gate = _rw_lora(xw, xa, xg, w["rw_w1"][j], w["rw_w2"][j], w["rw_a1"][j], w["rw_a2"][j],
                                    w["rw_g1"][j], w["rw_g2"][j], w["rw_w0"][j], w["rw_a0"][j], tm=min(m, 256))
            to_k = lambda z: _to_scan_k(z.reshape(bsz, seq, D_MODEL), g)
            y, s_new = _rw_scan(to_k(r), to_k(dec), to_k(k), to_k(a), _to_scan_v(v.reshape(bsz, seq, D_MODEL), g),
                                _weight_scan_k(w["rw_k_k"][j]), _weight_scan_k(w["rw_k_a"][j]),
                                _weight_scan_k(w["rw_r_k"][j].reshape(-1)),
                                _weight_scan_v(w["rw_ln_w"][j]), _weight_scan_v(w["rw_ln_b"][j]),
                                _state_to_scan(rw_s0[j], g), tt=min(seq, 64))
            new_rw.append(_state_from_scan(s_new, g))
            y = _from_scan_v(y, g).reshape(m, D_MODEL)
            x = _mm_ln(y, w["rw_w_o"][j], x, lw[0], lb[0], tm=tm, tk=D_MODEL, gate=gate)
        x, xb = _cross_attn(x, mem_k[i], mem_v[i], mem_cols[0], mem_cols[1], w["ca_w_q"][i], w["ca_w_o"][i],
                            lw[1], lb[1], batch=bsz, tm=min(seq, 512))
        c0v, c0g = _split_conv_state(conv0[i])
        if prompt:
            to_carry = lambda c: jnp.pad(c, ((0, 0), (SUBLANES - c.shape[1], 0), (0, 0)))
            act, csv, csg = _ffn_up(xb, w["ffn_up_v"][i], w["ffn_up_g"][i], w["ffn_cw_v"][i], w["ffn_cw_g"][i],
                                    w["ffn_cb_v"][i], w["ffn_cb_g"][i], to_carry(c0v), to_carry(c0g),
                                    tm=tm, tn=512, shift=1, seqs=bsz)
            conv_new = jnp.concatenate([csv[:, -2:, :FFN_HIDDEN], csg[:, -2:, :FFN_HIDDEN]], axis=-1)
        else:
            tmaj = lambda c: c.transpose(1, 0, 2).reshape(1, 2 * bsz, FFN_PAD)
            xt = xb.reshape(bsz, seq, D_MODEL).transpose(1, 0, 2).reshape(m, D_MODEL)
            act, csv, csg = _ffn_up(xt, w["ffn_up_v"][i], w["ffn_up_g"][i], w["ffn_cw_v"][i], w["ffn_cw_g"][i],
                                    w["ffn_cb_v"][i], w["ffn_cb_g"][i], tmaj(c0v), tmaj(c0g),
                                    tm=m, tn=512, shift=bsz, seqs=1)
            act = act.reshape(seq, bsz, FFN_PAD).transpose(1, 0, 2).reshape(m, FFN_PAD)
            unmaj = lambda c: c.reshape(2, bsz, FFN_PAD).transpose(1, 0, 2)[:, :, :FFN_HIDDEN]
            conv_new = jnp.concatenate([unmaj(csv), unmaj(csg)], axis=-1)
        new_conv.append(conv_new)
        x = _mm_ln(act, w["ffn_down"][i], x, lw[2], lb[2], tm=tm, tk=FFN_PAD // 4)
    return (x.reshape(bsz, seq, D_MODEL), jnp.stack(new_k), jnp.stack(new_v), jnp.stack(new_gla),
            jnp.stack(new_rw), jnp.stack(new_shift), jnp.stack(new_conv))


def kernel(x_prompt, x_sample, mem_prompt, cache_sb_k, cache_sb_v, page_table, state_gla, state_rwkv,
           state_rwkv_shift, state_ffn_conv, cache_mem_k, cache_mem_v, mx_w_in, sb_bias, gla_w_a2, gla_b_a,
           gla_norm_w, mx_w_out, rw_mix, rw_w_rkv, rw_w0, rw_w1, rw_w2, rw_a0, rw_a1, rw_a2, rw_g1, rw_g2,
           rw_k_k, rw_k_a, rw_r_k, rw_ln_w, rw_ln_b, rw_w_o, ca_w_q, ca_w_kv, ca_w_o, ffn_w_up, ffn_conv_w,
           ffn_conv_b, ffn_w_down, ln_w, ln_b):
    w = _prep_weights(dict(
        mx_w_in=mx_w_in, sb_bias=sb_bias, gla_w_a2=gla_w_a2, gla_b_a=gla_b_a, gla_norm_w=gla_norm_w,
        mx_w_out=mx_w_out, rw_mix=rw_mix, rw_w_rkv=rw_w_rkv, rw_w0=rw_w0, rw_w1=rw_w1, rw_w2=rw_w2,
        rw_a0=rw_a0, rw_a1=rw_a1, rw_a2=rw_a2, rw_g1=rw_g1, rw_g2=rw_g2, rw_k_k=rw_k_k, rw_k_a=rw_k_a,
        rw_r_k=rw_r_k, rw_ln_w=rw_ln_w, rw_ln_b=rw_ln_b, rw_w_o=rw_w_o, ca_w_q=ca_w_q, ca_w_kv=ca_w_kv,
        ca_w_o=ca_w_o, ffn_w_up=ffn_w_up, ffn_conv_w=ffn_conv_w, ffn_conv_b=ffn_conv_b,
        ffn_w_down=ffn_w_down, ln_w=ln_w, ln_b=ln_b))
    n_even = (DEPTH + 1) // 2
    n_odd = DEPTH // 2
    b, _, _ = x_prompt.shape
    db = x_sample.shape[0]

    mem_rows = mem_prompt.reshape(b * N_MEM, D_MODEL)
    mem_kv = [_mm(mem_rows, w["ca_w_kv"][i], 2 * MEM_WIDTH, tm=b * N_MEM, tn=512).reshape(b, N_MEM, 2 * MEM_WIDTH)
              for i in range(DEPTH)]
    mem_shape = (DEPTH, b, N_MEM, MEM_HEADS, MEM_HEAD_DIM)
    mem_k_p = jnp.stack([kv[..., :MEM_WIDTH] for kv in mem_kv]).reshape(mem_shape)
    mem_v_p = jnp.stack([kv[..., MEM_WIDTH:] for kv in mem_kv]).reshape(mem_shape)
    gla0 = jnp.zeros((n_even, b, GLA_HEADS, GLA_DK, GLA_DV), F32)
    rw0 = jnp.zeros((n_odd, b, RW_HEADS, RW_HEAD, RW_HEAD), F32)
    sh0 = jnp.zeros((n_odd, b, D_MODEL), F32)
    cv0 = jnp.zeros((DEPTH, b, 2, 2 * FFN_HIDDEN), F32)
    y_p, sbk_p, sbv_p, gla_p, rw_p, sh_p, conv_p = _trunk(
        x_prompt, mem_kv, mem_kv, (0, 1), None, None, gla0, rw0, sh0, cv0, w, prompt=True)

    mk = cache_mem_k.reshape(DEPTH, db, N_MEM, MEM_WIDTH)
    mv = cache_mem_v.reshape(DEPTH, db, N_MEM, MEM_WIDTH)
    y_s, sbk_s, sbv_s, gla_s, rw_s, sh_s, conv_s = _trunk(
        x_sample, mk, mv, (0, 0), (cache_sb_k, cache_sb_v), page_table, state_gla, state_rwkv,
        state_rwkv_shift, state_ffn_conv, w, prompt=False)
    return (y_p, y_s, sbk_p, sbv_p, gla_p, rw_p, sh_p, conv_p, mem_k_p, mem_v_p,
            sbk_s, sbv_s, gla_s, rw_s, sh_s, conv_s)
```

```python
import functools

import jax
import jax.numpy as jnp
from jax import lax
from jax.experimental import pallas as pl
from jax.experimental.pallas import tpu as pltpu

F32 = jnp.float32
BF16 = jnp.bfloat16

D_MODEL = 2048
DEPTH = 2
PAGE_SIZE = 128
SB_HEADS = 8
SB_HEAD_DIM = 128
SB_WIDTH = SB_HEADS * SB_HEAD_DIM
GLA_HEADS = 4
GLA_DK = 128
GLA_DV = 256
GLA_K_WIDTH = GLA_HEADS * GLA_DK
GLA_V_WIDTH = GLA_HEADS * GLA_DV
GLA_GATE_RANK = 16
GLA_GATE_PAD = 128
GLA_GATE_NORMALIZER = 16.0
GLA_CHUNK = 64
GLA_NORM_EPS = 1e-5
EVEN_MAIN_WIDTH = 3 * SB_WIDTH + 2 * GLA_K_WIDTH + 2 * GLA_V_WIDTH
RW_HEAD = 64
RW_HEADS = D_MODEL // RW_HEAD
RW_LN_EPS = 64e-5
RW_RANK_PAD = 128
N_MEM = 256
MEM_HEADS = 4
MEM_HEAD_DIM = 128
MEM_WIDTH = MEM_HEADS * MEM_HEAD_DIM
FFN_HIDDEN = 5504
FFN_PAD = 5632
LN_EPS = 1e-5
ALPHA = (2.0 * DEPTH) ** 0.25

LANES = 128
SUBLANES = 8
VMEM_LIMIT_MB = 56

NT_DIMS = (((1,), (1,)), ((), ()))
TN_DIMS = (((0,), (0,)), ((), ()))


def _row_dtype(rows):
    return BF16 if rows % (2 * SUBLANES) == 0 else F32


def _cp(*sem):
    return pltpu.CompilerParams(dimension_semantics=sem, vmem_limit_bytes=VMEM_LIMIT_MB * 1024 * 1024)


def _dot(a, b):
    return jnp.dot(a, b, preferred_element_type=F32)


def _dot_nt(a, b):
    return lax.dot_general(a, b, NT_DIMS, preferred_element_type=F32)


def _softplus_neg_abs(z):
    return jnp.log1p(jnp.exp(-jnp.abs(z)))


def _log_sigmoid(z):
    return jnp.minimum(z, 0.0) - _softplus_neg_abs(z)


def _sigmoid(z):
    return 1.0 / (1.0 + jnp.exp(-z))


def _split_bf16(x):
    hi = x.astype(BF16)
    lo = (x - hi.astype(F32)).astype(BF16)
    return hi, lo


def _layer_norm(y, w, b):
    mu = jnp.mean(y, axis=-1, keepdims=True)
    d = y - mu
    var = jnp.mean(d * d, axis=-1, keepdims=True)
    return d * lax.rsqrt(var + LN_EPS) * w + b


def _mm_kernel(x_ref, w_ref, o_ref, wb_ref):
    @pl.when(pl.program_id(1) == 0)
    def _():
        wb_ref[...] = w_ref[...].astype(BF16)

    o_ref[...] = _dot(x_ref[...].astype(BF16), wb_ref[...]).astype(o_ref.dtype)


def _mm(x, w, lead, n_out, *, tm, tn, out_dtype=F32):
    m, k = x.shape
    return pl.pallas_call(
        _mm_kernel,
        grid=(n_out // tn, m // tm),
        in_specs=[pl.BlockSpec((tm, k), lambda j, i: (i, 0)),
                  pl.BlockSpec((None, k, tn), lambda j, i: (lead, 0, j))],
        out_specs=pl.BlockSpec((tm, tn), lambda j, i: (i, j)),
        out_shape=jax.ShapeDtypeStruct((m, n_out), out_dtype),
        scratch_shapes=[pltpu.VMEM((k, tn), BF16)],
        compiler_params=_cp("parallel", "arbitrary"),
        name="mm",
    )(x, w)


def _mm_ln_kernel(*refs, nk, gated):
    if gated:
        x_ref, g_ref, w_ref, res_ref, lw_ref, lb_ref, o_ref = refs[:7]
        x = (x_ref[...] * g_ref[...]).astype(BF16)
    else:
        x_ref, w_ref, res_ref, lw_ref, lb_ref, o_ref = refs[:6]
        x = x_ref[...].astype(BF16)
    part = _dot(x, w_ref[...])

    def finish(h):
        o_ref[...] = _layer_norm(ALPHA * res_ref[...] + h, lw_ref[...], lb_ref[...])

    if nk == 1:
        finish(part)
        return
    acc_ref = refs[-1]
    kk = pl.program_id(1)

    @pl.when(kk == 0)
    def _():
        acc_ref[...] = part

    @pl.when(kk > 0)
    def _():
        acc_ref[...] += part

    @pl.when(kk == nk - 1)
    def _():
        finish(acc_ref[...])


def _mm_ln(x, w, res, lw, lb, *, tm, tk, gate=None):
    m, k = x.shape
    n = w.shape[1]
    nk = k // tk
    gated = gate is not None
    xs = pl.BlockSpec((tm, tk), lambda i, kk: (i, kk))
    row = pl.BlockSpec((tm, n), lambda i, kk: (i, 0))
    vec = pl.BlockSpec((1, n), lambda i, kk: (0, 0))
    in_specs = [xs] + ([xs] if gated else []) + [pl.BlockSpec((tk, n), lambda i, kk: (kk, 0)), row, vec, vec]
    args = [x] + ([gate] if gated else []) + [w, res, lw.reshape(1, n), lb.reshape(1, n)]
    return pl.pallas_call(
        functools.partial(_mm_ln_kernel, nk=nk, gated=gated),
        grid=(m // tm, nk),
        in_specs=in_specs,
        out_specs=row,
        out_shape=jax.ShapeDtypeStruct((m, n), F32),
        scratch_shapes=[pltpu.VMEM((tm, n), F32)] if nk > 1 else [],
        compiler_params=_cp("parallel", "arbitrary"),
        name="mm_ln",
    )(*args)


def _glu_ln_kernel(cv_ref, cg_ref, w_ref, res_ref, lw_ref, lb_ref, o_ref, *, kchunk):
    k = cv_ref.shape[1]
    h = None
    for c0 in range(0, k, kchunk):
        c1 = min(c0 + kchunk, k)
        g = cg_ref[:, c0:c1].astype(F32)
        act = (g * _sigmoid(g) * cv_ref[:, c0:c1].astype(F32)).astype(BF16)
        part = _dot(act, w_ref[c0:c1, :])
        h = part if h is None else h + part
    o_ref[...] = _layer_norm(ALPHA * res_ref[...] + h, lw_ref[...], lb_ref[...])


def _glu_ln(c, w, lead, res, lw, lb, *, tm, kchunk):
    m = c.shape[0]
    _, k, n = w.shape
    row = pl.BlockSpec((tm, n), lambda i: (i, 0))
    vec = pl.BlockSpec((1, n), lambda i: (0, 0))
    return pl.pallas_call(
        functools.partial(_glu_ln_kernel, kchunk=kchunk),
        grid=(m // tm,),
        in_specs=[pl.BlockSpec((tm, k), lambda i: (i, 0)), pl.BlockSpec((tm, k), lambda i: (i, 1)),
                  pl.BlockSpec((None, k, n), lambda i: (lead, 0, 0), pipeline_mode=pl.Buffered(1)),
                  row, vec, vec],
        out_specs=row,
        out_shape=jax.ShapeDtypeStruct((m, n), F32),
        compiler_params=_cp("parallel"),
        name="glu_ln",
    )(c, c, w, res, lw.reshape(1, n), lb.reshape(1, n))


def _ca_kernel(x_ref, wq_ref, mk_ref, mv_ref, wo_ref, lw_ref, lb_ref, o_ref, ob_ref):
    x = x_ref[...]
    q = _dot(x.astype(BF16), wq_ref[...])
    heads = []
    for h in range(MEM_HEADS):
        cols = slice(h * MEM_HEAD_DIM, (h + 1) * MEM_HEAD_DIM)
        s = _dot_nt(q[:, cols].astype(BF16), mk_ref[:, cols].astype(BF16)) * (MEM_HEAD_DIM ** -0.5)
        e = jnp.exp(s - jnp.max(s, axis=-1, keepdims=True))
        p = e / jnp.sum(e, axis=-1, keepdims=True)
        heads.append(_dot(p.astype(BF16), mv_ref[:, cols].astype(BF16)))
    o = jnp.concatenate(heads, axis=-1)
    y = _layer_norm(ALPHA * x + _dot(o.astype(BF16), wo_ref[...]), lw_ref[...], lb_ref[...])
    o_ref[...] = y
    ob_ref[...] = y.astype(ob_ref.dtype)


def _cross_attn(x, mem_k, mem_v, k_col, v_col, wq, wo, lw, lb, *, batch, tm):
    m = x.shape[0]
    tiles = m // batch // tm
    vec = pl.BlockSpec((1, D_MODEL), lambda b, i: (0, 0))
    row = pl.BlockSpec((tm, D_MODEL), lambda b, i: (b * tiles + i, 0))
    return pl.pallas_call(
        _ca_kernel,
        grid=(batch, tiles),
        in_specs=[row,
                  pl.BlockSpec((D_MODEL, MEM_WIDTH), lambda b, i: (0, 0)),
                  pl.BlockSpec((None, N_MEM, MEM_WIDTH), lambda b, i: (b, 0, k_col)),
                  pl.BlockSpec((None, N_MEM, MEM_WIDTH), lambda b, i: (b, 0, v_col)),
                  pl.BlockSpec((MEM_WIDTH, D_MODEL), lambda b, i: (0, 0)),
                  vec, vec],
        out_specs=[row, row],
        out_shape=[jax.ShapeDtypeStruct((m, D_MODEL), F32), jax.ShapeDtypeStruct((m, D_MODEL), _row_dtype(tm))],
        compiler_params=_cp("parallel", "arbitrary"),
        name="cross_attn",
    )(x, wq, mem_k, mem_v, wo, lw.reshape(1, -1), lb.reshape(1, -1))


def _ffn_up_kernel(x_ref, w_ref, cw_ref, cb_ref, c0_ref, c_ref, cs_ref, wb_ref, buf,
                   *, tm, shift, carry_rows, tiles_per_seq):
    i = pl.program_id(1)
    cr = carry_rows

    @pl.when(i == 0)
    def _():
        wb_ref[...] = w_ref[...].astype(BF16)

    @pl.when(i % tiles_per_seq == 0)
    def _():
        buf[0:cr, :] = c0_ref[...]

    buf[cr:cr + tm, :] = _dot(x_ref[...].astype(BF16), wb_ref[...])
    c = (cw_ref[0:1, :] * buf[cr - 2 * shift:cr - 2 * shift + tm, :]
         + cw_ref[1:2, :] * buf[cr - shift:cr - shift + tm, :]
         + cw_ref[2:3, :] * buf[cr:cr + tm, :] + cb_ref[...])
    c_ref[...] = c.astype(c_ref.dtype)
    last = buf[tm:tm + cr, :]
    cs_ref[...] = last
    buf[0:cr, :] = last


def _ffn_up(xb, w, lead, cw, cb, c0, *, tm, tn, shift, seqs):
    m, k = xb.shape
    n = w.shape[2]
    carry_rows = c0.shape[1]
    tiles_per_seq = m // seqs // tm
    col = lambda j, i: (0, j)
    st = pl.BlockSpec((None, carry_rows, tn), lambda j, i: (i // tiles_per_seq, 0, j))
    return pl.pallas_call(
        functools.partial(_ffn_up_kernel, tm=tm, shift=shift, carry_rows=carry_rows, tiles_per_seq=tiles_per_seq),
        grid=(n // tn, m // tm),
        in_specs=[pl.BlockSpec((tm, k), lambda j, i: (i, 0)),
                  pl.BlockSpec((None, k, tn), lambda j, i: (lead, 0, j)),
                  pl.BlockSpec((3, tn), col), pl.BlockSpec((1, tn), col), st],
        out_specs=[pl.BlockSpec((tm, tn), lambda j, i: (i, j)), st],
        out_shape=[jax.ShapeDtypeStruct((m, n), BF16), jax.ShapeDtypeStruct((seqs, carry_rows, n), F32)],
        scratch_shapes=[pltpu.VMEM((k, tn), BF16), pltpu.VMEM((tm + carry_rows, tn), F32)],
        compiler_params=_cp("parallel", "arbitrary"),
        name="ffn_up",
    )(xb, w, cw, cb, c0)


def _sb_weights(z, carry, tri, mask):
    sp = jnp.log(1.0 + jnp.exp(-jnp.abs(z)))
    log_beta = jnp.minimum(z, 0.0) - sp
    log_keep = jnp.minimum(-z, 0.0) - sp
    if mask is not None:
        log_keep = jnp.where(mask, log_keep, 0.0)
    hi, lo = _split_bf16(log_keep)
    a = jnp.exp(log_beta + _dot(hi, tri) + _dot(lo, tri) + carry)
    if mask is not None:
        a = jnp.where(mask, a, 0.0)
    return a, carry + jnp.sum(log_keep, axis=1, keepdims=True)


def _sb_prompt_kernel(bias_ref, q_ref, k_ref, v_ref, o_ref, *, tq, tk):
    h = pl.program_id(1)
    qi = pl.program_id(2)
    bias = bias_ref[h]
    nsub = tq // tk
    q = (q_ref[...] * (SB_HEAD_DIM ** -0.5)).astype(BF16)
    r = lax.broadcasted_iota(jnp.int32, (tk, tk), 0)
    c = lax.broadcasted_iota(jnp.int32, (tk, tk), 1)
    tri = jnp.where(r > c, 1.0, 0.0).astype(BF16)
    trow = lax.broadcasted_iota(jnp.int32, (tq, tk), 0)
    scol = lax.broadcasted_iota(jnp.int32, (tq, tk), 1)

    def block(start, acc, carry, mask):
        k = k_ref[pl.ds(start, tk), :].astype(BF16)
        v = v_ref[pl.ds(start, tk), :].astype(BF16)
        a, carry = _sb_weights(_dot_nt(q, k) + bias, carry, tri, mask)
        return acc + _dot(a.astype(BF16), v), carry

    acc = jnp.zeros((tq, SB_HEAD_DIM), F32)
    carry = jnp.zeros((tq, 1), F32)
    for d in reversed(range(nsub)):
        start = pl.multiple_of(qi * tq + d * tk, tk)
        acc, carry = block(start, acc, carry, (scol + d * tk) < trow)

    def body(it, state):
        acc, carry = state
        for d in range(nsub):
            kb = (qi - it) * nsub - 1 - d
            acc, carry = block(pl.multiple_of(kb * tk, tk), acc, carry, None)
        return acc, carry

    acc, carry = lax.fori_loop(0, qi, body, (acc, carry))
    o_ref[...] = acc.astype(o_ref.dtype)


def _sb_prompt(proj, bias, *, batch, seq, tq, tk):
    nq = seq // tq
    grid_spec = pltpu.PrefetchScalarGridSpec(
        num_scalar_prefetch=1,
        grid=(batch, SB_HEADS, nq),
        in_specs=[pl.BlockSpec((tq, SB_HEAD_DIM), lambda b, h, i, s: (b * nq + i, h)),
                  pl.BlockSpec((seq, SB_HEAD_DIM), lambda b, h, i, s: (b, SB_HEADS + h)),
                  pl.BlockSpec((seq, SB_HEAD_DIM), lambda b, h, i, s: (b, 2 * SB_HEADS + h))],
        out_specs=pl.BlockSpec((tq, SB_HEAD_DIM), lambda b, h, i, s: (b * nq + i, h)),
    )
    return pl.pallas_call(
        functools.partial(_sb_prompt_kernel, tq=tq, tk=tk),
        grid_spec=grid_spec,
        out_shape=jax.ShapeDtypeStruct((batch * seq, SB_WIDTH), BF16),
        compiler_params=_cp("parallel", "parallel", "arbitrary"),
        name="sb_prompt",
    )(bias, proj, proj, proj)


def _sb_sample_kernel(pt_ref, q_ref, bias_ref, kn_ref, vn_ref, *rest, pages):
    k_refs, v_refs = rest[:pages], rest[pages:2 * pages]
    o_ref, acc_ref, carry_ref = rest[2 * pages:]
    s = pl.program_id(1)
    n = LANES
    q = q_ref[...]
    row = lax.broadcasted_iota(jnp.int32, (n, n), 0)
    col = lax.broadcasted_iota(jnp.int32, (n, n), 1)
    tri = jnp.where(col > row, 1.0, 0.0).astype(BF16)
    col_head = col // SUBLANES
    col_t = col % SUBLANES
    bias = bias_ref[...]

    def head_rows(ref):
        return jnp.concatenate([ref[pl.ds(h, PAGE_SIZE, stride=SB_HEADS), :].astype(BF16)
                                for h in range(SB_HEADS)], axis=0)

    def attend(k_list, v_list, mask, acc, carry):
        ps = PAGE_SIZE
        zs = []
        for k_ref in k_list:
            zz = _dot_nt(head_rows(k_ref), q)
            z = zz[0:ps, :]
            for h in range(1, SB_HEADS):
                z = jnp.where(col_head == h, zz[h * ps:(h + 1) * ps, :], z)
            zs.append(z)
        z = jnp.concatenate(zs, axis=0) + bias
        sp = jnp.log(1.0 + jnp.exp(-jnp.abs(z)))
        log_beta = jnp.minimum(z, 0.0) - sp
        log_keep = jnp.minimum(-z, 0.0) - sp
        if mask is not None:
            log_keep = jnp.where(mask, log_keep, 0.0)
        hi, lo = _split_bf16(log_keep)
        np_ = len(k_list)
        rhs = jnp.concatenate([x[p * ps:(p + 1) * ps, :] for p in range(np_) for x in (hi, lo)], axis=1)
        both = _dot(tri, rhs)
        ats = []
        for p in range(np_):
            rs = slice(p * ps, (p + 1) * ps)
            tail = both[:, 2 * p * n:(2 * p + 1) * n] + both[:, (2 * p + 1) * n:(2 * p + 2) * n]
            a = jnp.exp(log_beta[rs, :] + tail + carry)
            if mask is not None:
                a = jnp.where(mask, a, 0.0)
            carry = carry + jnp.sum(log_keep[rs, :], axis=0, keepdims=True)
            ats.append(a.T.astype(BF16))
        at = jnp.concatenate(ats, axis=1)
        vs = [head_rows(v_ref) for v_ref in v_list]
        outs = []
        for h in range(SB_HEADS):
            vh = jnp.concatenate([v[h * ps:(h + 1) * ps, :] for v in vs], axis=0)
            outs.append(_dot(at[h * SUBLANES:(h + 1) * SUBLANES, :], vh))
        return acc + jnp.concatenate(outs, axis=0), carry

    @pl.when(s == 0)
    def _():
        acc0, carry0 = attend([kn_ref], [vn_ref], row < col_t,
                              jnp.zeros(acc_ref.shape, F32), jnp.zeros((1, n), F32))
        acc_ref[...] = acc0
        carry_ref[...] = carry0

    acc, carry = attend(k_refs, v_refs, None, acc_ref[...], carry_ref[...])
    acc_ref[...] = acc
    carry_ref[...] = carry

    @pl.when(s == pl.num_programs(1) - 1)
    def _():
        o_ref[...] = acc


def _sb_sample(q_rows, bias_cols, k_new, v_new, cache_k, cache_v, page_table, *, pages):
    nb, n_pages = page_table.shape
    rows = PAGE_SIZE * SB_HEADS
    out_rows = SB_HEADS * SUBLANES

    def page_spec(p):
        return pl.BlockSpec((None, rows, SB_HEAD_DIM),
                            lambda b, s, pt: (pt[b, n_pages - 1 - (s * pages + p)], 0, 0))

    per_b = lambda b, s, pt: (b, 0, 0)
    grid_spec = pltpu.PrefetchScalarGridSpec(
        num_scalar_prefetch=1,
        grid=(nb, n_pages // pages),
        in_specs=[pl.BlockSpec((None, LANES, SB_HEAD_DIM), per_b),
                  pl.BlockSpec((1, LANES), lambda b, s, pt: (0, 0)),
                  pl.BlockSpec((None, rows, SB_HEAD_DIM), per_b),
                  pl.BlockSpec((None, rows, SB_HEAD_DIM), per_b)]
        + [page_spec(p) for p in range(pages)] * 2,
        out_specs=pl.BlockSpec((None, out_rows, SB_HEAD_DIM), per_b),
        scratch_shapes=[pltpu.VMEM((out_rows, SB_HEAD_DIM), F32), pltpu.VMEM((1, LANES), F32)],
    )
    return pl.pallas_call(
        functools.partial(_sb_sample_kernel, pages=pages),
        grid_spec=grid_spec,
        out_shape=jax.ShapeDtypeStruct((nb, out_rows, SB_HEAD_DIM), F32),
        compiler_params=_cp("parallel", "arbitrary"),
        name="sb_sample",
    )(page_table, q_rows, bias_cols, k_new, v_new, *([cache_k] * pages), *([cache_v] * pages))


def _gla_kernel(q_ref, k_ref, v_ref, g_ref, low_ref, wa_ref, ba_ref, nw_ref, s0_ref, o_ref, sout_ref, st_ref,
                *, rows, nsteps):
    si = pl.program_id(1)
    c = GLA_CHUNK
    rp = max(rows, c)
    chunks = rp // c

    @pl.when(si == 0)
    def _():
        st_ref[...] = s0_ref[...]

    def pad(x):
        if rows == rp:
            return x
        return jnp.concatenate([x, jnp.zeros((rp - rows, x.shape[1]), x.dtype)], axis=0)

    r = lax.broadcasted_iota(jnp.int32, (rp, rp), 0)
    cc = lax.broadcasted_iota(jnp.int32, (rp, rp), 1)
    causal = (r >= cc) & ((r // c) == (cc // c))
    log_a = _log_sigmoid(_dot(pad(low_ref[...]).astype(BF16), wa_ref[...]) + ba_ref[...])
    log_a = log_a * (1.0 / GLA_GATE_NORMALIZER)
    if rows < rp:
        log_a = jnp.where(lax.broadcasted_iota(jnp.int32, log_a.shape, 0) < rows, log_a, 0.0)
    hi, lo = _split_bf16(log_a)
    ltri = jnp.where(causal, 1.0, 0.0).astype(BF16)
    b = _dot(ltri, hi) + _dot(ltri, lo)
    b_last = [b[(ci + 1) * c - 1:(ci + 1) * c, :] for ci in range(chunks)]
    b_end = jnp.concatenate([jnp.broadcast_to(bl, (c, bl.shape[1])) for bl in b_last], axis=0)
    k = pad(k_ref[...])
    q_dec = (pad(q_ref[...]) * (GLA_DK ** -0.5) * jnp.exp(b)).astype(BF16)
    k_inv = (k * jnp.exp(-b)).astype(BF16)
    k_state = (k * jnp.exp(b_end - b)).astype(BF16)
    v = pad(v_ref[...]).astype(BF16)
    g = pad(g_ref[...])
    outs = []
    for h in range(GLA_HEADS):
        kc = slice(h * GLA_DK, (h + 1) * GLA_DK)
        vc = slice(h * GLA_DV, (h + 1) * GLA_DV)
        scores = _dot_nt(q_dec[:, kc], k_inv[:, kc])
        o_intra = _dot(jnp.where(causal, scores, 0.0).astype(BF16), v[:, vc])
        st = st_ref[h]
        o_inter = []
        for ci in range(chunks):
            rs = slice(ci * c, (ci + 1) * c)
            o_inter.append(_dot_nt(q_dec[rs, kc], st.astype(BF16)))
            st = st * jnp.exp(b_last[ci][:, kc]) + lax.dot_general(v[rs, vc], k_state[rs, kc], TN_DIMS,
                                                                   preferred_element_type=F32)
        st_ref[h] = st
        o = o_intra + jnp.concatenate(o_inter, axis=0)
        o = o * lax.rsqrt(jnp.mean(o * o, axis=-1, keepdims=True) + GLA_NORM_EPS) * nw_ref[...]
        gh = g[:, vc]
        outs.append(o * (gh * _sigmoid(gh)))
    o_ref[...] = jnp.concatenate(outs, axis=1)[:rows, :].astype(o_ref.dtype)

    @pl.when(si == nsteps - 1)
    def _():
        sout_ref[...] = st_ref[...]


def _gla(proj, low, wa, ba, nw, s0t, *, batch, seq, rows):
    nsteps = seq // rows
    qc = 3 * SB_WIDTH // GLA_K_WIDTH
    vc = (3 * SB_WIDTH + 2 * GLA_K_WIDTH) // GLA_V_WIDTH
    rowblk = lambda b, i: b * nsteps + i
    st = pl.BlockSpec((None, GLA_HEADS, GLA_DV, GLA_DK), lambda b, i: (b, 0, 0, 0))
    return pl.pallas_call(
        functools.partial(_gla_kernel, rows=rows, nsteps=nsteps),
        grid=(batch, nsteps),
        in_specs=[pl.BlockSpec((rows, GLA_K_WIDTH), lambda b, i: (rowblk(b, i), qc)),
                  pl.BlockSpec((rows, GLA_K_WIDTH), lambda b, i: (rowblk(b, i), qc + 1)),
                  pl.BlockSpec((rows, GLA_V_WIDTH), lambda b, i: (rowblk(b, i), vc)),
                  pl.BlockSpec((rows, GLA_V_WIDTH), lambda b, i: (rowblk(b, i), vc + 1)),
                  pl.BlockSpec((rows, GLA_GATE_PAD), lambda b, i: (rowblk(b, i), 0)),
                  pl.BlockSpec((GLA_GATE_PAD, GLA_K_WIDTH), lambda b, i: (0, 0)),
                  pl.BlockSpec((1, GLA_K_WIDTH), lambda b, i: (0, 0)),
                  pl.BlockSpec((1, GLA_DV), lambda b, i: (0, 0)),
                  st],
        out_specs=[pl.BlockSpec((rows, GLA_V_WIDTH), lambda b, i: (rowblk(b, i), 0)), st],
        out_shape=[jax.ShapeDtypeStruct((batch * seq, GLA_V_WIDTH), _row_dtype(rows)),
                   jax.ShapeDtypeStruct((batch, GLA_HEADS, GLA_DV, GLA_DK), F32)],
        scratch_shapes=[pltpu.VMEM((GLA_HEADS, GLA_DV, GLA_DK), F32)],
        compiler_params=_cp("parallel", "arbitrary"),
        name="gla",
    )(proj, proj, proj, proj, low, wa, ba.reshape(1, -1), nw.reshape(1, -1), s0t)


def _rw_mix_kernel(x_ref, xp_ref, mix_ref, *o_refs):
    x = x_ref[...]
    xx = xp_ref[...] - x
    for m, o_ref in enumerate(o_refs):
        o_ref[...] = (x + xx * mix_ref[m:m + 1, :]).astype(o_ref.dtype)


def _rw_mix(x, x_prev, mix, *, tm):
    m, d = x.shape
    row = pl.BlockSpec((tm, d), lambda i: (i, 0))
    return pl.pallas_call(
        _rw_mix_kernel,
        grid=(m // tm,),
        in_specs=[row, row, pl.BlockSpec((6, d), lambda i: (0, 0))],
        out_specs=[row] * 6,
        out_shape=[jax.ShapeDtypeStruct((m, d), BF16)] * 6,
        compiler_params=_cp("parallel"),
        name="rw_mix",
    )(x, x_prev, mix)


def _rw_lora_kernel(xw_ref, xa_ref, xg_ref, w1_ref, w2_ref, a1_ref, a2_ref, g1_ref, g2_ref, w0_ref, a0_ref,
                    dec_ref, a_ref, g_ref):
    hw = jnp.tanh(_dot(xw_ref[...], w1_ref[...]))
    wl = w0_ref[...] + _dot(hw.astype(BF16), w2_ref[...])
    w_log = _log_sigmoid(wl) - 0.5
    dec_ref[...] = jnp.exp(-jnp.exp(w_log))
    ha = _dot(xa_ref[...], a1_ref[...])
    a_ref[...] = _sigmoid(a0_ref[...] + _dot(ha.astype(BF16), a2_ref[...]))
    hg = _sigmoid(_dot(xg_ref[...], g1_ref[...]))
    g_ref[...] = _dot(hg.astype(BF16), g2_ref[...])


def _rw_lora(xw, xa, xg, w1, w2, a1, a2, g1, g2, w0, a0, *, tm):
    m, d = xw.shape
    row = pl.BlockSpec((tm, d), lambda i: (i, 0))
    full = lambda a: pl.BlockSpec(a.shape, lambda i: (0, 0))
    w0 = w0.reshape(1, d)
    a0 = a0.reshape(1, d)
    consts = [w1, w2, a1, a2, g1, g2, w0, a0]
    return pl.pallas_call(
        _rw_lora_kernel,
        grid=(m // tm,),
        in_specs=[row, row, row] + [full(a) for a in consts],
        out_specs=[row] * 3,
        out_shape=[jax.ShapeDtypeStruct((m, d), F32)] * 3,
        compiler_params=_cp("parallel"),
        name="rw_lora",
    )(xw, xa, xg, *consts)


def _rw_scan_kernel(r_ref, w_ref, k_ref, a_ref, v_ref, kkw_ref, kaw_ref, rkw_ref, lnw_ref, lnb_ref, s0_ref,
                    y_ref, sout_ref, s_ref, kk_ref, bb_ref, km_ref, *, tt, nblk):
    tb = pl.program_id(1)
    n = RW_HEAD

    @pl.when(tb == 0)
    def _():
        s_ref[...] = s0_ref[...]

    k = k_ref[...]
    a = a_ref[...]
    kk = k * kkw_ref[...]
    kk = kk * lax.rsqrt(jnp.maximum(jnp.sum(kk * kk, axis=1, keepdims=True), 1e-24))
    km = k * (1.0 + (a - 1.0) * kaw_ref[...])
    kk_ref[...] = kk
    bb_ref[...] = kk * a
    km_ref[...] = km
    bonus = jnp.sum(r_ref[...] * km * rkw_ref[...], axis=1, keepdims=True)

    sa0 = jnp.zeros((n // 2, LANES), F32)
    for j in range(n):
        sa0 = sa0 - s_ref[j] * kk_ref[0, j:j + 1, :]

    def step(t, sa):
        t_next = jnp.minimum(t + 1, tt - 1)
        vv = v_ref[t]
        y_acc = jnp.zeros((n // 2, LANES), F32)
        sa_acc = jnp.zeros((n // 2, LANES), F32)
        for j in range(n):
            sj = (s_ref[j] * w_ref[t, j:j + 1, :] + sa * bb_ref[t, j:j + 1, :] + vv * km_ref[t, j:j + 1, :])
            s_ref[j] = sj
            y_acc = y_acc + sj * r_ref[t, j:j + 1, :]
            sa_acc = sa_acc - sj * kk_ref[t_next, j:j + 1, :]
        y_ref[t] = y_acc
        return sa_acc

    lax.fori_loop(0, tt, step, sa0)

    y = y_ref[...]

    def head_mean(x):
        part = jnp.sum(x, axis=1, keepdims=True)
        return (part + pltpu.roll(part, LANES // 2, axis=2)) * (1.0 / n)

    d = y - head_mean(y)
    yn = d * lax.rsqrt(head_mean(d * d) + RW_LN_EPS) * lnw_ref[...] + lnb_ref[...]
    y_ref[...] = yn + bonus * v_ref[...]

    @pl.when(tb == nblk - 1)
    def _():
        sout_ref[...] = s_ref[...]


def _rw_scan(r, w, k, a, v, kkw, kaw, rkw, lnw, lnb, s0, *, tt):
    g, t, n, _ = r.shape
    nblk = t // tt
    kblk = pl.BlockSpec((None, tt, n, LANES), lambda gi, ti: (gi, ti, 0, 0))
    vblk = pl.BlockSpec((None, tt, n // 2, LANES), lambda gi, ti: (gi, ti, 0, 0))
    kconst = pl.BlockSpec((n, LANES), lambda gi, ti: (0, 0))
    vconst = pl.BlockSpec((n // 2, LANES), lambda gi, ti: (0, 0))
    sblk = pl.BlockSpec((None, n, n // 2, LANES), lambda gi, ti: (gi, 0, 0, 0))
    return pl.pallas_call(
        functools.partial(_rw_scan_kernel, tt=tt, nblk=nblk),
        grid=(g, nblk),
        in_specs=[kblk, kblk, kblk, kblk, vblk, kconst, kconst, kconst, vconst, vconst, sblk],
        out_specs=[vblk, sblk],
        out_shape=[jax.ShapeDtypeStruct((g, t, n // 2, LANES), F32),
                   jax.ShapeDtypeStruct((g, n, n // 2, LANES), F32)],
        scratch_shapes=[pltpu.VMEM((n, n // 2, LANES), F32)] + [pltpu.VMEM((tt, n, LANES), F32)] * 3,
        compiler_params=_cp("parallel", "arbitrary"),
        name="rw_scan",
    )(r, w, k, a, v, kkw, kaw, rkw, lnw, lnb, s0)


def _to_scan_k(x, g):
    b, t, _ = x.shape
    x = x.reshape(g, 2, t, RW_HEADS, RW_HEAD).transpose(0, 2, 4, 1, 3).reshape(g, t, RW_HEAD, 2 * RW_HEADS)
    return jnp.concatenate([x, x], axis=-1)


def _to_scan_v(x, g):
    b, t, _ = x.shape
    x = x.reshape(g, 2, t, RW_HEADS, 2, RW_HEAD // 2).transpose(0, 2, 5, 4, 1, 3)
    return x.reshape(g, t, RW_HEAD // 2, LANES)


def _from_scan_v(y, g):
    _, t, _, _ = y.shape
    y = y.reshape(g, t, RW_HEAD // 2, 2, 2, RW_HEADS).transpose(0, 4, 1, 5, 3, 2)
    return y.reshape(g * 2, t, D_MODEL)


def _weight_scan_k(w):
    return jnp.tile(w.reshape(RW_HEADS, RW_HEAD).T, (1, 4))


def _weight_scan_v(w):
    w = w.reshape(RW_HEADS, 2, RW_HEAD // 2).transpose(2, 1, 0)[:, :, None, :]
    return jnp.broadcast_to(w, (RW_HEAD // 2, 2, 2, RW_HEADS)).reshape(RW_HEAD // 2, LANES)


def _state_to_scan(s, g):
    s = s.reshape(g, 2, RW_HEADS, 2, RW_HEAD // 2, RW_HEAD).transpose(0, 5, 4, 3, 1, 2)
    return s.reshape(g, RW_HEAD, RW_HEAD // 2, LANES)


def _state_from_scan(s, g):
    s = s.reshape(g, RW_HEAD, RW_HEAD // 2, 2, 2, RW_HEADS).transpose(0, 4, 5, 3, 2, 1)
    return s.reshape(g * 2, RW_HEADS, RW_HEAD, RW_HEAD)


def _prep_weights(p):
    w = {}
    w["mx_w_in"] = p["mx_w_in"]
    gpad = GLA_GATE_PAD - GLA_GATE_RANK
    w["mx_w_low"] = jnp.pad(p["mx_w_in"][:, :, EVEN_MAIN_WIDTH:], ((0, 0), (0, 0), (0, gpad))).astype(BF16)
    w["gla_w_a2"] = jnp.pad(p["gla_w_a2"], ((0, 0), (0, gpad), (0, 0))).astype(BF16)
    w["mx_w_out"] = p["mx_w_out"].astype(BF16)
    w["rw_w_rkv"] = p["rw_w_rkv"].reshape(-1, D_MODEL, D_MODEL)
    pad_c = lambda a: jnp.pad(a, ((0, 0), (0, 0), (0, RW_RANK_PAD - a.shape[2]))).astype(BF16)
    pad_r = lambda a: jnp.pad(a, ((0, 0), (0, RW_RANK_PAD - a.shape[1]), (0, 0))).astype(BF16)
    w["rw_w1"], w["rw_w2"] = pad_c(p["rw_w1"]), pad_r(p["rw_w2"])
    w["rw_a1"], w["rw_a2"] = pad_c(p["rw_a1"]), pad_r(p["rw_a2"])
    w["rw_g1"], w["rw_g2"] = p["rw_g1"].astype(BF16), p["rw_g2"].astype(BF16)
    w["rw_w_o"] = p["rw_w_o"].astype(BF16)
    w["ca_w_q"] = p["ca_w_q"].astype(BF16)
    w["ca_w_kv"] = p["ca_w_kv"]
    w["ca_w_o"] = p["ca_w_o"].astype(BF16)
    w["ffn_w_up"] = p["ffn_w_up"]
    w["ffn_conv_b"] = p["ffn_conv_b"][:, None, :]
    w["ffn_down"] = p["ffn_w_down"].astype(BF16)
    for name in ("sb_bias", "gla_b_a", "gla_norm_w", "rw_mix", "rw_w0", "rw_a0", "rw_k_k", "rw_k_a", "rw_r_k",
                 "rw_ln_w", "rw_ln_b", "ffn_conv_w", "ln_w", "ln_b"):
        w[name] = p[name]
    return w


def _trunk(x, mem_k, mem_v, mem_cols, sb_past, page_table, gla_s0, rw_s0, rw_shift0, conv0, w, *, prompt):
    bsz, seq, _ = x.shape
    m = bsz * seq
    tm = 512 if prompt else m
    x = x.reshape(m, D_MODEL)
    new_k, new_v, new_gla, new_rw, new_shift, new_conv = [], [], [], [], [], []
    for i in range(DEPTH):
        j = i // 2
        lw, lb = w["ln_w"][i], w["ln_b"][i]
        if i % 2 == 0:
            proj = _mm(x, w["mx_w_in"], j, EVEN_MAIN_WIDTH, tm=min(m, 1024), tn=768)
            low = _mm(x, w["mx_w_low"], j, GLA_GATE_PAD, tm=min(m, 1024), tn=GLA_GATE_PAD)
            ka = proj[:, SB_WIDTH:2 * SB_WIDTH].reshape(bsz, seq, SB_HEADS, SB_HEAD_DIM)
            va = proj[:, 2 * SB_WIDTH:3 * SB_WIDTH].reshape(bsz, seq, SB_HEADS, SB_HEAD_DIM)
            new_k.append(ka)
            new_v.append(va)
            if prompt:
                oa = _sb_prompt(proj, w["sb_bias"][j], batch=bsz, seq=seq, tq=512, tk=256)
            else:
                cache_k, cache_v = sb_past
                n_phys = cache_k.shape[1]
                rows = PAGE_SIZE * SB_HEADS
                qa = proj[:, :SB_WIDTH].reshape(bsz, seq, SB_HEADS, SB_HEAD_DIM).transpose(0, 2, 1, 3)
                qa = (qa.reshape(bsz, SB_HEADS * seq, SB_HEAD_DIM) * (SB_HEAD_DIM ** -0.5)).astype(BF16)
                qa = jnp.pad(qa, ((0, 0), (0, LANES - SB_HEADS * seq), (0, 0)))
                bias_cols = jnp.pad(jnp.repeat(w["sb_bias"][j], seq), (0, LANES - SB_HEADS * seq)).reshape(1, LANES)
                pad_new = lambda a: jnp.pad(a.reshape(bsz, seq * SB_HEADS, SB_HEAD_DIM),
                                            ((0, 0), (0, rows - seq * SB_HEADS), (0, 0)))
                oa = _sb_sample(qa, bias_cols, pad_new(ka), pad_new(va),
                                cache_k[j].reshape(n_phys, rows, SB_HEAD_DIM),
                                cache_v[j].reshape(n_phys, rows, SB_HEAD_DIM), page_table, pages=8)
                oa = oa[:, :SB_HEADS * seq].reshape(bsz, SB_HEADS, seq, SB_HEAD_DIM).transpose(0, 2, 1, 3)
                oa = oa.reshape(m, SB_WIDTH).astype(BF16)
            ob, s_t = _gla(proj, low, w["gla_w_a2"][j], w["gla_b_a"][j], w["gla_norm_w"][j],
                           gla_s0[j].transpose(0, 1, 3, 2), batch=bsz, seq=seq, rows=min(seq, 256))
            new_gla.append(s_t.transpose(0, 1, 3, 2))
            o = jnp.concatenate([oa, ob.astype(BF16)], axis=-1)
            x = _mm_ln(o, w["mx_w_out"][j], x, lw[0], lb[0], tm=tm, tk=o.shape[1])
        else:
            g = bsz // 2
            x3 = x.reshape(bsz, seq, D_MODEL)
            new_shift.append(x3[:, -1])
            x_prev = jnp.concatenate([rw_shift0[j][:, None, :], x3[:, :-1]], axis=1).reshape(m, D_MODEL)
            xr, xw, xk, xv, xa, xg = _rw_mix(x, x_prev, w["rw_mix"][j], tm=tm)
            r, k, v = (_mm(xi, w["rw_w_rkv"], 3 * j + n, D_MODEL, tm=min(m, 1024), tn=1024)
                       for n, xi in enumerate((xr, xk, xv)))
            dec, a, gate = _rw_lora(xw, xa, xg, w["rw_w1"][j], w["rw_w2"][j], w["rw_a1"][j], w["rw_a2"][j],
                                    w["rw_g1"][j], w["rw_g2"][j], w["rw_w0"][j], w["rw_a0"][j], tm=min(m, 256))
            to_k = lambda z: _to_scan_k(z.reshape(bsz, seq, D_MODEL), g)
            y, s_new = _rw_scan(to_k(r), to_k(dec), to_k(k), to_k(a), _to_scan_v(v.reshape(bsz, seq, D_MODEL), g),
                                _weight_scan_k(w["rw_k_k"][j]), _weight_scan_k(w["rw_k_a"][j]),
                                _weight_scan_k(w["rw_r_k"][j].reshape(-1)),
                                _weight_scan_v(w["rw_ln_w"][j]), _weight_scan_v(w["rw_ln_b"][j]),
                                _state_to_scan(rw_s0[j], g), tt=min(seq, 64))
            new_rw.append(_state_from_scan(s_new, g))
            y = _from_scan_v(y, g).reshape(m, D_MODEL)
            x = _mm_ln(y, w["rw_w_o"][j], x, lw[0], lb[0], tm=tm, tk=D_MODEL, gate=gate)
        x, xb = _cross_attn(x, mem_k[i], mem_v[i], mem_cols[0], mem_cols[1], w["ca_w_q"][i], w["ca_w_o"][i],
                            lw[1], lb[1], batch=bsz, tm=min(seq, 512))
        f2 = 2 * FFN_HIDDEN
        cw, cb = w["ffn_conv_w"][i], w["ffn_conv_b"][i]
        if prompt:
            c0 = jnp.pad(conv0[i], ((0, 0), (SUBLANES - conv0[i].shape[1], 0), (0, 0)))
            c, cs = _ffn_up(xb, w["ffn_w_up"], i, cw, cb, c0, tm=min(seq, 2048), tn=256, shift=1, seqs=bsz)
            new_conv.append(cs[:, -2:, :])
        else:
            c0 = conv0[i].transpose(1, 0, 2).reshape(1, 2 * bsz, f2)
            xt = xb.reshape(bsz, seq, D_MODEL).transpose(1, 0, 2).reshape(m, D_MODEL)
            c, cs = _ffn_up(xt, w["ffn_w_up"], i, cw, cb, c0, tm=m, tn=256, shift=bsz, seqs=1)
            c = c.reshape(seq, bsz, f2).transpose(1, 0, 2).reshape(m, f2)
            new_conv.append(cs.reshape(2, bsz, f2).transpose(1, 0, 2))
        x = _glu_ln(c, w["ffn_down"], i, x, lw[2], lb[2], tm=min(m, 256), kchunk=1408)
    return (x.reshape(bsz, seq, D_MODEL), jnp.stack(new_k), jnp.stack(new_v), jnp.stack(new_gla),
            jnp.stack(new_rw), jnp.stack(new_shift), jnp.stack(new_conv))


def kernel(x_prompt, x_sample, mem_prompt, cache_sb_k, cache_sb_v, page_table, state_gla, state_rwkv,
           state_rwkv_shift, state_ffn_conv, cache_mem_k, cache_mem_v, mx_w_in, sb_bias, gla_w_a2, gla_b_a,
           gla_norm_w, mx_w_out, rw_mix, rw_w_rkv, rw_w0, rw_w1, rw_w2, rw_a0, rw_a1, rw_a2, rw_g1, rw_g2,
           rw_k_k, rw_k_a, rw_r_k, rw_ln_w, rw_ln_b, rw_w_o, ca_w_q, ca_w_kv, ca_w_o, ffn_w_up, ffn_conv_w,
           ffn_conv_b, ffn_w_down, ln_w, ln_b):
    w = _prep_weights(dict(
        mx_w_in=mx_w_in, sb_bias=sb_bias, gla_w_a2=gla_w_a2, gla_b_a=gla_b_a, gla_norm_w=gla_norm_w,
        mx_w_out=mx_w_out, rw_mix=rw_mix, rw_w_rkv=rw_w_rkv, rw_w0=rw_w0, rw_w1=rw_w1, rw_w2=rw_w2,
        rw_a0=rw_a0, rw_a1=rw_a1, rw_a2=rw_a2, rw_g1=rw_g1, rw_g2=rw_g2, rw_k_k=rw_k_k, rw_k_a=rw_k_a,
        rw_r_k=rw_r_k, rw_ln_w=rw_ln_w, rw_ln_b=rw_ln_b, rw_w_o=rw_w_o, ca_w_q=ca_w_q, ca_w_kv=ca_w_kv,
        ca_w_o=ca_w_o, ffn_w_up=ffn_w_up, ffn_conv_w=ffn_conv_w, ffn_conv_b=ffn_conv_b,
        ffn_w_down=ffn_w_down, ln_w=ln_w, ln_b=ln_b))
    n_even = (DEPTH + 1) // 2
    n_odd = DEPTH // 2
    b, _, _ = x_prompt.shape
    db = x_sample.shape[0]

    mem_rows = mem_prompt.reshape(b * N_MEM, D_MODEL)
    mem_kv = [_mm(mem_rows, w["ca_w_kv"], i, 2 * MEM_WIDTH, tm=b * N_MEM, tn=512).reshape(b, N_MEM, 2 * MEM_WIDTH)
              for i in range(DEPTH)]
    mem_shape = (DEPTH, b, N_MEM, MEM_HEADS, MEM_HEAD_DIM)
    mem_k_p = jnp.stack([kv[..., :MEM_WIDTH] for kv in mem_kv]).reshape(mem_shape)
    mem_v_p = jnp.stack([kv[..., MEM_WIDTH:] for kv in mem_kv]).reshape(mem_shape)
    gla0 = jnp.zeros((n_even, b, GLA_HEADS, GLA_DK, GLA_DV), F32)
    rw0 = jnp.zeros((n_odd, b, RW_HEADS, RW_HEAD, RW_HEAD), F32)
    sh0 = jnp.zeros((n_odd, b, D_MODEL), F32)
    cv0 = jnp.zeros((DEPTH, b, 2, 2 * FFN_HIDDEN), F32)
    y_p, sbk_p, sbv_p, gla_p, rw_p, sh_p, conv_p = _trunk(
        x_prompt, mem_kv, mem_kv, (0, 1), None, None, gla0, rw0, sh0, cv0, w, prompt=True)

    mk = cache_mem_k.reshape(DEPTH, db, N_MEM, MEM_WIDTH)
    mv = cache_mem_v.reshape(DEPTH, db, N_MEM, MEM_WIDTH)
    y_s, sbk_s, sbv_s, gla_s, rw_s, sh_s, conv_s = _trunk(
        x_sample, mk, mv, (0, 0), (cache_sb_k, cache_sb_v), page_table, state_gla, state_rwkv,
        state_rwkv_shift, state_ffn_conv, w, prompt=False)
    return (y_p, y_s, sbk_p, sbv_p, gla_p, rw_p, sh_p, conv_p, mem_k_p, mem_v_p,
            sbk_s, sbv_s, gla_s, rw_s, sh_s, conv_s)
```

```python
import functools

import jax
import jax.numpy as jnp
from jax import lax
from jax.experimental import pallas as pl
from jax.experimental.pallas import tpu as pltpu

F32 = jnp.float32
BF16 = jnp.bfloat16

D_MODEL = 2048
DEPTH = 2
PAGE_SIZE = 128
SB_HEADS = 8
SB_HEAD_DIM = 128
SB_WIDTH = SB_HEADS * SB_HEAD_DIM
GLA_HEADS = 4
GLA_DK = 128
GLA_DV = 256
GLA_K_WIDTH = GLA_HEADS * GLA_DK
GLA_V_WIDTH = GLA_HEADS * GLA_DV
GLA_GATE_RANK = 16
GLA_GATE_PAD = 128
GLA_GATE_NORMALIZER = 16.0
GLA_CHUNK = 64
GLA_NORM_EPS = 1e-5
EVEN_MAIN_WIDTH = 3 * SB_WIDTH + 2 * GLA_K_WIDTH + 2 * GLA_V_WIDTH
RW_HEAD = 64
RW_HEADS = D_MODEL // RW_HEAD
RW_LN_EPS = 64e-5
RW_RANK_PAD = 128
N_MEM = 256
MEM_HEADS = 4
MEM_HEAD_DIM = 128
MEM_WIDTH = MEM_HEADS * MEM_HEAD_DIM
FFN_HIDDEN = 5504
FFN_PAD = 5632
LN_EPS = 1e-5
ALPHA = (2.0 * DEPTH) ** 0.25

LANES = 128
SUBLANES = 8
VMEM_LIMIT_MB = 56

NT_DIMS = (((1,), (1,)), ((), ()))
TN_DIMS = (((0,), (0,)), ((), ()))


def _row_dtype(rows):
    return BF16 if rows % (2 * SUBLANES) == 0 else F32


def _cp(*sem):
    return pltpu.CompilerParams(dimension_semantics=sem, vmem_limit_bytes=VMEM_LIMIT_MB * 1024 * 1024)


def _dot(a, b):
    return jnp.dot(a, b, preferred_element_type=F32)


def _dot_nt(a, b):
    return lax.dot_general(a, b, NT_DIMS, preferred_element_type=F32)


def _softplus_neg_abs(z):
    return jnp.log1p(jnp.exp(-jnp.abs(z)))


def _log_sigmoid(z):
    return jnp.minimum(z, 0.0) - _softplus_neg_abs(z)


def _sigmoid(z):
    return 1.0 / (1.0 + jnp.exp(-z))


def _split_bf16(x):
    hi = x.astype(BF16)
    lo = (x - hi.astype(F32)).astype(BF16)
    return hi, lo


def _layer_norm(y, w, b):
    mu = jnp.mean(y, axis=-1, keepdims=True)
    d = y - mu
    var = jnp.mean(d * d, axis=-1, keepdims=True)
    return d * lax.rsqrt(var + LN_EPS) * w + b


def _mm_kernel(x_ref, w_ref, o_ref, wb_ref):
    @pl.when(pl.program_id(1) == 0)
    def _():
        wb_ref[...] = w_ref[...].astype(BF16)

    o_ref[...] = _dot(x_ref[...].astype(BF16), wb_ref[...]).astype(o_ref.dtype)


def _mm(x, w, lead, n_out, *, tm, tn, out_dtype=F32):
    m, k = x.shape
    return pl.pallas_call(
        _mm_kernel,
        grid=(n_out // tn, m // tm),
        in_specs=[pl.BlockSpec((tm, k), lambda j, i: (i, 0)),
                  pl.BlockSpec((None, k, tn), lambda j, i: (lead, 0, j))],
        out_specs=pl.BlockSpec((tm, tn), lambda j, i: (i, j)),
        out_shape=jax.ShapeDtypeStruct((m, n_out), out_dtype),
        scratch_shapes=[pltpu.VMEM((k, tn), BF16)],
        compiler_params=_cp("parallel", "arbitrary"),
        name="mm",
    )(x, w)


def _mm_ln_kernel(*refs, nk, gated):
    if gated:
        x_ref, g_ref, w_ref, res_ref, lw_ref, lb_ref, o_ref = refs[:7]
        x = (x_ref[...] * g_ref[...]).astype(BF16)
    else:
        x_ref, w_ref, res_ref, lw_ref, lb_ref, o_ref = refs[:6]
        x = x_ref[...].astype(BF16)
    part = _dot(x, w_ref[...])

    def finish(h):
        o_ref[...] = _layer_norm(ALPHA * res_ref[...] + h, lw_ref[...], lb_ref[...])

    if nk == 1:
        finish(part)
        return
    acc_ref = refs[-1]
    kk = pl.program_id(1)

    @pl.when(kk == 0)
    def _():
        acc_ref[...] = part

    @pl.when(kk > 0)
    def _():
        acc_ref[...] += part

    @pl.when(kk == nk - 1)
    def _():
        finish(acc_ref[...])


def _mm_ln(x, w, res, lw, lb, *, tm, tk, gate=None):
    m, k = x.shape
    n = w.shape[1]
    nk = k // tk
    gated = gate is not None
    xs = pl.BlockSpec((tm, tk), lambda i, kk: (i, kk))
    row = pl.BlockSpec((tm, n), lambda i, kk: (i, 0))
    vec = pl.BlockSpec((1, n), lambda i, kk: (0, 0))
    in_specs = [xs] + ([xs] if gated else []) + [pl.BlockSpec((tk, n), lambda i, kk: (kk, 0)), row, vec, vec]
    args = [x] + ([gate] if gated else []) + [w, res, lw.reshape(1, n), lb.reshape(1, n)]
    return pl.pallas_call(
        functools.partial(_mm_ln_kernel, nk=nk, gated=gated),
        grid=(m // tm, nk),
        in_specs=in_specs,
        out_specs=row,
        out_shape=jax.ShapeDtypeStruct((m, n), F32),
        scratch_shapes=[pltpu.VMEM((tm, n), F32)] if nk > 1 else [],
        compiler_params=_cp("parallel", "arbitrary"),
        name="mm_ln",
    )(*args)


def _glu_ln_kernel(cv_ref, cg_ref, w_ref, res_ref, lw_ref, lb_ref, o_ref, *, kchunk):
    k = cv_ref.shape[1]
    h = None
    for c0 in range(0, k, kchunk):
        c1 = min(c0 + kchunk, k)
        g = cg_ref[:, c0:c1].astype(F32)
        act = (g * _sigmoid(g) * cv_ref[:, c0:c1].astype(F32)).astype(BF16)
        part = _dot(act, w_ref[c0:c1, :])
        h = part if h is None else h + part
    o_ref[...] = _layer_norm(ALPHA * res_ref[...] + h, lw_ref[...], lb_ref[...])


def _glu_ln(c, w, lead, res, lw, lb, *, tm, kchunk):
    m = c.shape[0]
    _, k, n = w.shape
    row = pl.BlockSpec((tm, n), lambda i: (i, 0))
    vec = pl.BlockSpec((1, n), lambda i: (0, 0))
    return pl.pallas_call(
        functools.partial(_glu_ln_kernel, kchunk=kchunk),
        grid=(m // tm,),
        in_specs=[pl.BlockSpec((tm, k), lambda i: (i, 0)), pl.BlockSpec((tm, k), lambda i: (i, 1)),
                  pl.BlockSpec((None, k, n), lambda i: (lead, 0, 0), pipeline_mode=pl.Buffered(1)),
                  row, vec, vec],
        out_specs=row,
        out_shape=jax.ShapeDtypeStruct((m, n), F32),
        compiler_params=_cp("parallel"),
        name="glu_ln",
    )(c, c, w, res, lw.reshape(1, n), lb.reshape(1, n))


def _ca_kernel(x_ref, wq_ref, mk_ref, mv_ref, wo_ref, lw_ref, lb_ref, o_ref, ob_ref):
    x = x_ref[...]
    q = _dot(x.astype(BF16), wq_ref[...])
    heads = []
    for h in range(MEM_HEADS):
        cols = slice(h * MEM_HEAD_DIM, (h + 1) * MEM_HEAD_DIM)
        s = _dot_nt(q[:, cols].astype(BF16), mk_ref[:, cols].astype(BF16)) * (MEM_HEAD_DIM ** -0.5)
        e = jnp.exp(s - jnp.max(s, axis=-1, keepdims=True))
        p = e / jnp.sum(e, axis=-1, keepdims=True)
        heads.append(_dot(p.astype(BF16), mv_ref[:, cols].astype(BF16)))
    o = jnp.concatenate(heads, axis=-1)
    y = _layer_norm(ALPHA * x + _dot(o.astype(BF16), wo_ref[...]), lw_ref[...], lb_ref[...])
    o_ref[...] = y
    ob_ref[...] = y.astype(ob_ref.dtype)


def _cross_attn(x, mem_k, mem_v, k_col, v_col, wq, wo, lw, lb, *, batch, tm):
    m = x.shape[0]
    tiles = m // batch // tm
    vec = pl.BlockSpec((1, D_MODEL), lambda b, i: (0, 0))
    row = pl.BlockSpec((tm, D_MODEL), lambda b, i: (b * tiles + i, 0))
    return pl.pallas_call(
        _ca_kernel,
        grid=(batch, tiles),
        in_specs=[row,
                  pl.BlockSpec((D_MODEL, MEM_WIDTH), lambda b, i: (0, 0)),
                  pl.BlockSpec((None, N_MEM, MEM_WIDTH), lambda b, i: (b, 0, k_col)),
                  pl.BlockSpec((None, N_MEM, MEM_WIDTH), lambda b, i: (b, 0, v_col)),
                  pl.BlockSpec((MEM_WIDTH, D_MODEL), lambda b, i: (0, 0)),
                  vec, vec],
        out_specs=[row, row],
        out_shape=[jax.ShapeDtypeStruct((m, D_MODEL), F32), jax.ShapeDtypeStruct((m, D_MODEL), _row_dtype(tm))],
        compiler_params=_cp("parallel", "arbitrary"),
        name="cross_attn",
    )(x, wq, mem_k, mem_v, wo, lw.reshape(1, -1), lb.reshape(1, -1))


def _ffn_up_kernel(x_ref, w_ref, cw_ref, cb_ref, c0_ref, c_ref, cs_ref, wb_ref, buf,
                   *, tm, shift, carry_rows, tiles_per_seq):
    i = pl.program_id(1)
    cr = carry_rows

    @pl.when(i == 0)
    def _():
        wb_ref[...] = w_ref[...].astype(BF16)

    @pl.when(i % tiles_per_seq == 0)
    def _():
        buf[0:cr, :] = c0_ref[...]

    buf[cr:cr + tm, :] = _dot(x_ref[...].astype(BF16), wb_ref[...])
    c = (cw_ref[0:1, :] * buf[cr - 2 * shift:cr - 2 * shift + tm, :]
         + cw_ref[1:2, :] * buf[cr - shift:cr - shift + tm, :]
         + cw_ref[2:3, :] * buf[cr:cr + tm, :] + cb_ref[...])
    c_ref[...] = c.astype(c_ref.dtype)
    last = buf[tm:tm + cr, :]
    cs_ref[...] = last
    buf[0:cr, :] = last


def _ffn_up(xb, w, lead, cw, cb, c0, *, tm, tn, shift, seqs):
    m, k = xb.shape
    n = w.shape[2]
    carry_rows = c0.shape[1]
    tiles_per_seq = m // seqs // tm
    col = lambda j, i: (0, j)
    st = pl.BlockSpec((None, carry_rows, tn), lambda j, i: (i // tiles_per_seq, 0, j))
    return pl.pallas_call(
        functools.partial(_ffn_up_kernel, tm=tm, shift=shift, carry_rows=carry_rows, tiles_per_seq=tiles_per_seq),
        grid=(n // tn, m // tm),
        in_specs=[pl.BlockSpec((tm, k), lambda j, i: (i, 0)),
                  pl.BlockSpec((None, k, tn), lambda j, i: (lead, 0, j)),
                  pl.BlockSpec((3, tn), col), pl.BlockSpec((1, tn), col), st],
        out_specs=[pl.BlockSpec((tm, tn), lambda j, i: (i, j)), st],
        out_shape=[jax.ShapeDtypeStruct((m, n), BF16), jax.ShapeDtypeStruct((seqs, carry_rows, n), F32)],
        scratch_shapes=[pltpu.VMEM((k, tn), BF16), pltpu.VMEM((tm + carry_rows, tn), F32)],
        compiler_params=_cp("parallel", "arbitrary"),
        name="ffn_up",
    )(xb, w, cw, cb, c0)


def _sb_weights(z, carry, tri, mask):
    sp = jnp.log(1.0 + jnp.exp(-jnp.abs(z)))
    log_beta = jnp.minimum(z, 0.0) - sp
    log_keep = jnp.minimum(-z, 0.0) - sp
    if mask is not None:
        log_keep = jnp.where(mask, log_keep, 0.0)
    hi, lo = _split_bf16(log_keep)
    a = jnp.exp(log_beta + _dot(hi, tri) + _dot(lo, tri) + carry)
    if mask is not None:
        a = jnp.where(mask, a, 0.0)
    return a, carry + jnp.sum(log_keep, axis=1, keepdims=True)


def _sb_prompt_kernel(bias_ref, q_ref, k_ref, v_ref, o_ref, *, tq, tk):
    h = pl.program_id(1)
    qi = pl.program_id(2)
    bias = bias_ref[h]
    nsub = tq // tk
    q = (q_ref[...] * (SB_HEAD_DIM ** -0.5)).astype(BF16)
    r = lax.broadcasted_iota(jnp.int32, (tk, tk), 0)
    c = lax.broadcasted_iota(jnp.int32, (tk, tk), 1)
    tri = jnp.where(r > c, 1.0, 0.0).astype(BF16)
    trow = lax.broadcasted_iota(jnp.int32, (tq, tk), 0)
    scol = lax.broadcasted_iota(jnp.int32, (tq, tk), 1)

    def block(start, acc, carry, mask):
        k = k_ref[pl.ds(start, tk), :].astype(BF16)
        v = v_ref[pl.ds(start, tk), :].astype(BF16)
        a, carry = _sb_weights(_dot_nt(q, k) + bias, carry, tri, mask)
        return acc + _dot(a.astype(BF16), v), carry

    acc = jnp.zeros((tq, SB_HEAD_DIM), F32)
    carry = jnp.zeros((tq, 1), F32)
    for d in reversed(range(nsub)):
        start = pl.multiple_of(qi * tq + d * tk, tk)
        acc, carry = block(start, acc, carry, (scol + d * tk) < trow)

    def body(it, state):
        acc, carry = state
        for d in range(nsub):
            kb = (qi - it) * nsub - 1 - d
            acc, carry = block(pl.multiple_of(kb * tk, tk), acc, carry, None)
        return acc, carry

    acc, carry = lax.fori_loop(0, qi, body, (acc, carry))
    o_ref[...] = acc.astype(o_ref.dtype)


def _sb_prompt(proj, bias, *, batch, seq, tq, tk):
    nq = seq // tq
    grid_spec = pltpu.PrefetchScalarGridSpec(
        num_scalar_prefetch=1,
        grid=(batch, SB_HEADS, nq),
        in_specs=[pl.BlockSpec((tq, SB_HEAD_DIM), lambda b, h, i, s: (b * nq + i, h)),
                  pl.BlockSpec((seq, SB_HEAD_DIM), lambda b, h, i, s: (b, SB_HEADS + h)),
                  pl.BlockSpec((seq, SB_HEAD_DIM), lambda b, h, i, s: (b, 2 * SB_HEADS + h))],
        out_specs=pl.BlockSpec((tq, SB_HEAD_DIM), lambda b, h, i, s: (b * nq + i, h)),
    )
    return pl.pallas_call(
        functools.partial(_sb_prompt_kernel, tq=tq, tk=tk),
        grid_spec=grid_spec,
        out_shape=jax.ShapeDtypeStruct((batch * seq, SB_WIDTH), BF16),
        compiler_params=_cp("parallel", "parallel", "arbitrary"),
        name="sb_prompt",
    )(bias, proj, proj, proj)


def _sb_sample_kernel(pt_ref, q_ref, bias_ref, kn_ref, vn_ref, *rest, pages):
    k_refs, v_refs = rest[:pages], rest[pages:2 * pages]
    o_ref, acc_ref, carry_ref = rest[2 * pages:]
    s = pl.program_id(1)
    n = LANES
    q = q_ref[...]
    row = lax.broadcasted_iota(jnp.int32, (n, n), 0)
    col = lax.broadcasted_iota(jnp.int32, (n, n), 1)
    tri = jnp.where(col > row, 1.0, 0.0).astype(BF16)
    col_head = col // SUBLANES
    col_t = col % SUBLANES
    bias = bias_ref[...]

    def head_rows(ref):
        return jnp.concatenate([ref[pl.ds(h, PAGE_SIZE, stride=SB_HEADS), :].astype(BF16)
                                for h in range(SB_HEADS)], axis=0)

    def attend(k_list, v_list, mask, acc, carry):
        ps = PAGE_SIZE
        zs = []
        for k_ref in k_list:
            zz = _dot_nt(head_rows(k_ref), q)
            z = zz[0:ps, :]
            for h in range(1, SB_HEADS):
                z = jnp.where(col_head == h, zz[h * ps:(h + 1) * ps, :], z)
            zs.append(z)
        z = jnp.concatenate(zs, axis=0) + bias
        sp = jnp.log(1.0 + jnp.exp(-jnp.abs(z)))
        log_beta = jnp.minimum(z, 0.0) - sp
        log_keep = jnp.minimum(-z, 0.0) - sp
        if mask is not None:
            log_keep = jnp.where(mask, log_keep, 0.0)
        hi, lo = _split_bf16(log_keep)
        np_ = len(k_list)
        rhs = jnp.concatenate([x[p * ps:(p + 1) * ps, :] for p in range(np_) for x in (hi, lo)], axis=1)
        both = _dot(tri, rhs)
        ats = []
        for p in range(np_):
            rs = slice(p * ps, (p + 1) * ps)
            tail = both[:, 2 * p * n:(2 * p + 1) * n] + both[:, (2 * p + 1) * n:(2 * p + 2) * n]
            a = jnp.exp(log_beta[rs, :] + tail + carry)
            if mask is not None:
                a = jnp.where(mask, a, 0.0)
            carry = carry + jnp.sum(log_keep[rs, :], axis=0, keepdims=True)
            ats.append(a.T.astype(BF16))
        at = jnp.concatenate(ats, axis=1)
        vs = [head_rows(v_ref) for v_ref in v_list]
        outs = []
        for h in range(SB_HEADS):
            vh = jnp.concatenate([v[h * ps:(h + 1) * ps, :] for v in vs], axis=0)
            outs.append(_dot(at[h * SUBLANES:(h + 1) * SUBLANES, :], vh))
        return acc + jnp.concatenate(outs, axis=0), carry

    @pl.when(s == 0)
    def _():
        acc0, carry0 = attend([kn_ref], [vn_ref], row < col_t,
                              jnp.zeros(acc_ref.shape, F32), jnp.zeros((1, n), F32))
        acc_ref[...] = acc0
        carry_ref[...] = carry0

    acc, carry = attend(k_refs, v_refs, None, acc_ref[...], carry_ref[...])
    acc_ref[...] = acc
    carry_ref[...] = carry

    @pl.when(s == pl.num_programs(1) - 1)
    def _():
        o_ref[...] = acc


def _sb_sample(q_rows, bias_cols, k_new, v_new, cache_k, cache_v, page_table, *, pages):
    nb, n_pages = page_table.shape
    rows = PAGE_SIZE * SB_HEADS
    out_rows = SB_HEADS * SUBLANES

    def page_spec(p):
        return pl.BlockSpec((None, rows, SB_HEAD_DIM),
                            lambda b, s, pt: (pt[b, n_pages - 1 - (s * pages + p)], 0, 0))

    per_b = lambda b, s, pt: (b, 0, 0)
    grid_spec = pltpu.PrefetchScalarGridSpec(
        num_scalar_prefetch=1,
        grid=(nb, n_pages // pages),
        in_specs=[pl.BlockSpec((None, LANES, SB_HEAD_DIM), per_b),
                  pl.BlockSpec((1, LANES), lambda b, s, pt: (0, 0)),
                  pl.BlockSpec((None, rows, SB_HEAD_DIM), per_b),
                  pl.BlockSpec((None, rows, SB_HEAD_DIM), per_b)]
        + [page_spec(p) for p in range(pages)] * 2,
        out_specs=pl.BlockSpec((None, out_rows, SB_HEAD_DIM), per_b),
        scratch_shapes=[pltpu.VMEM((out_rows, SB_HEAD_DIM), F32), pltpu.VMEM((1, LANES), F32)],
    )
    return pl.pallas_call(
        functools.partial(_sb_sample_kernel, pages=pages),
        grid_spec=grid_spec,
        out_shape=jax.ShapeDtypeStruct((nb, out_rows, SB_HEAD_DIM), F32),
        compiler_params=_cp("parallel", "arbitrary"),
        name="sb_sample",
    )(page_table, q_rows, bias_cols, k_new, v_new, *([cache_k] * pages), *([cache_v] * pages))


def _gla_kernel(q_ref, k_ref, v_ref, g_ref, low_ref, wa_ref, ba_ref, nw_ref, s0_ref, o_ref, sout_ref, st_ref,
                *, rows, nsteps):
    si = pl.program_id(1)
    c = GLA_CHUNK
    rp = max(rows, c)
    chunks = rp // c

    @pl.when(si == 0)
    def _():
        st_ref[...] = s0_ref[...]

    def pad(x):
        if rows == rp:
            return x
        return jnp.concatenate([x, jnp.zeros((rp - rows, x.shape[1]), x.dtype)], axis=0)

    r = lax.broadcasted_iota(jnp.int32, (rp, rp), 0)
    cc = lax.broadcasted_iota(jnp.int32, (rp, rp), 1)
    causal = (r >= cc) & ((r // c) == (cc // c))
    log_a = _log_sigmoid(_dot(pad(low_ref[...]).astype(BF16), wa_ref[...]) + ba_ref[...])
    log_a = log_a * (1.0 / GLA_GATE_NORMALIZER)
    if rows < rp:
        log_a = jnp.where(lax.broadcasted_iota(jnp.int32, log_a.shape, 0) < rows, log_a, 0.0)
    hi, lo = _split_bf16(log_a)
    ltri = jnp.where(causal, 1.0, 0.0).astype(BF16)
    b = _dot(ltri, hi) + _dot(ltri, lo)
    b_last = [b[(ci + 1) * c - 1:(ci + 1) * c, :] for ci in range(chunks)]
    b_end = jnp.concatenate([jnp.broadcast_to(bl, (c, bl.shape[1])) for bl in b_last], axis=0)
    k = pad(k_ref[...])
    q_dec = (pad(q_ref[...]) * (GLA_DK ** -0.5) * jnp.exp(b)).astype(BF16)
    k_inv = (k * jnp.exp(-b)).astype(BF16)
    k_state = (k * jnp.exp(b_end - b)).astype(BF16)
    v = pad(v_ref[...]).astype(BF16)
    g = pad(g_ref[...])
    outs = []
    for h in range(GLA_HEADS):
        kc = slice(h * GLA_DK, (h + 1) * GLA_DK)
        vc = slice(h * GLA_DV, (h + 1) * GLA_DV)
        scores = _dot_nt(q_dec[:, kc], k_inv[:, kc])
        o_intra = _dot(jnp.where(causal, scores, 0.0).astype(BF16), v[:, vc])
        st = st_ref[h]
        o_inter = []
        for ci in range(chunks):
            rs = slice(ci * c, (ci + 1) * c)
            o_inter.append(_dot_nt(q_dec[rs, kc], st.astype(BF16)))
            st = st * jnp.exp(b_last[ci][:, kc]) + lax.dot_general(v[rs, vc], k_state[rs, kc], TN_DIMS,
                                                                   preferred_element_type=F32)
        st_ref[h] = st
        o = o_intra + jnp.concatenate(o_inter, axis=0)
        o = o * lax.rsqrt(jnp.mean(o * o, axis=-1, keepdims=True) + GLA_NORM_EPS) * nw_ref[...]
        gh = g[:, vc]
        outs.append(o * (gh * _sigmoid(gh)))
    o_ref[...] = jnp.concatenate(outs, axis=1)[:rows, :].astype(o_ref.dtype)

    @pl.when(si == nsteps - 1)
    def _():
        sout_ref[...] = st_ref[...]


def _gla(proj, low, wa, ba, nw, s0t, *, batch, seq, rows):
    nsteps = seq // rows
    qc = 3 * SB_WIDTH // GLA_K_WIDTH
    vc = (3 * SB_WIDTH + 2 * GLA_K_WIDTH) // GLA_V_WIDTH
    rowblk = lambda b, i: b * nsteps + i
    st = pl.BlockSpec((None, GLA_HEADS, GLA_DV, GLA_DK), lambda b, i: (b, 0, 0, 0))
    return pl.pallas_call(
        functools.partial(_gla_kernel, rows=rows, nsteps=nsteps),
        grid=(batch, nsteps),
        in_specs=[pl.BlockSpec((rows, GLA_K_WIDTH), lambda b, i: (rowblk(b, i), qc)),
                  pl.BlockSpec((rows, GLA_K_WIDTH), lambda b, i: (rowblk(b, i), qc + 1)),
                  pl.BlockSpec((rows, GLA_V_WIDTH), lambda b, i: (rowblk(b, i), vc)),
                  pl.BlockSpec((rows, GLA_V_WIDTH), lambda b, i: (rowblk(b, i), vc + 1)),
                  pl.BlockSpec((rows, GLA_GATE_PAD), lambda b, i: (rowblk(b, i), 0)),
                  pl.BlockSpec((GLA_GATE_PAD, GLA_K_WIDTH), lambda b, i: (0, 0)),
                  pl.BlockSpec((1, GLA_K_WIDTH), lambda b, i: (0, 0)),
                  pl.BlockSpec((1, GLA_DV), lambda b, i: (0, 0)),
                  st],
        out_specs=[pl.BlockSpec((rows, GLA_V_WIDTH), lambda b, i: (rowblk(b, i), 0)), st],
        out_shape=[jax.ShapeDtypeStruct((batch * seq, GLA_V_WIDTH), _row_dtype(rows)),
                   jax.ShapeDtypeStruct((batch, GLA_HEADS, GLA_DV, GLA_DK), F32)],
        scratch_shapes=[pltpu.VMEM((GLA_HEADS, GLA_DV, GLA_DK), F32)],
        compiler_params=_cp("parallel", "arbitrary"),
        name="gla",
    )(proj, proj, proj, proj, low, wa, ba.reshape(1, -1), nw.reshape(1, -1), s0t)


def _rw_mix_kernel(x_ref, xp_ref, mix_ref, *o_refs):
    x = x_ref[...]
    xx = xp_ref[...] - x
    for m, o_ref in enumerate(o_refs):
        o_ref[...] = (x + xx * mix_ref[m:m + 1, :]).astype(o_ref.dtype)


def _rw_mix(x, x_prev, mix, *, tm):
    m, d = x.shape
    row = pl.BlockSpec((tm, d), lambda i: (i, 0))
    return pl.pallas_call(
        _rw_mix_kernel,
        grid=(m // tm,),
        in_specs=[row, row, pl.BlockSpec((6, d), lambda i: (0, 0))],
        out_specs=[row] * 6,
        out_shape=[jax.ShapeDtypeStruct((m, d), BF16)] * 6,
        compiler_params=_cp("parallel"),
        name="rw_mix",
    )(x, x_prev, mix)


def _rw_lora_kernel(xw_ref, xa_ref, xg_ref, w1_ref, w2_ref, a1_ref, a2_ref, g1_ref, g2_ref, w0_ref, a0_ref,
                    dec_ref, a_ref, g_ref):
    hw = jnp.tanh(_dot(xw_ref[...], w1_ref[...]))
    wl = w0_ref[...] + _dot(hw.astype(BF16), w2_ref[...])
    w_log = _log_sigmoid(wl) - 0.5
    dec_ref[...] = jnp.exp(-jnp.exp(w_log))
    ha = _dot(xa_ref[...], a1_ref[...])
    a_ref[...] = _sigmoid(a0_ref[...] + _dot(ha.astype(BF16), a2_ref[...]))
    hg = _sigmoid(_dot(xg_ref[...], g1_ref[...]))
    g_ref[...] = _dot(hg.astype(BF16), g2_ref[...])


def _rw_lora(xw, xa, xg, w1, w2, a1, a2, g1, g2, w0, a0, *, tm):
    m, d = xw.shape
    row = pl.BlockSpec((tm, d), lambda i: (i, 0))
    full = lambda a: pl.BlockSpec(a.shape, lambda i: (0, 0))
    w0 = w0.reshape(1, d)
    a0 = a0.reshape(1, d)
    consts = [w1, w2, a1, a2, g1, g2, w0, a0]
    return pl.pallas_call(
        _rw_lora_kernel,
        grid=(m // tm,),
        in_specs=[row, row, row] + [full(a) for a in consts],
        out_specs=[row] * 3,
        out_shape=[jax.ShapeDtypeStruct((m, d), F32)] * 3,
        compiler_params=_cp("parallel"),
        name="rw_lora",
    )(xw, xa, xg, *consts)


def _lane_group_sum(x):
    axis = x.ndim - 1
    x = x + pltpu.roll(x, RW_HEADS, axis=axis)
    return x + pltpu.roll(x, 2 * RW_HEADS, axis=axis)


def _rw_scan_kernel(r_ref, w_ref, k_ref, a_ref, v_ref, kkw_ref, kaw_ref, rkw_ref, lnw_ref, lnb_ref, s0_ref,
                    y_ref, sout_ref, s00_ref, s01_ref, s10_ref, s11_ref, kk_ref, bb_ref, km_ref, bonus_ref,
                    *, tt, nblk):
    tb = pl.program_id(1)
    ng = D_MODEL // LANES
    nh = RW_HEAD // 2
    halves = (slice(0, nh), slice(nh, RW_HEAD))
    state = ((s00_ref, s01_ref), (s10_ref, s11_ref))

    @pl.when(tb == 0)
    def _():
        for b in range(2):
            for h in range(2):
                state[b][h][...] = s0_ref[b, :, halves[h], :]

    def head_total(x):
        return _lane_group_sum(jnp.sum(x, axis=2, keepdims=True))

    for c0 in range(0, tt, SUBLANES):
        ts = slice(c0, min(c0 + SUBLANES, tt))
        k = k_ref[:, ts]
        a = a_ref[:, ts]
        kk = k * kkw_ref[...]
        kk = kk * lax.rsqrt(jnp.maximum(head_total(kk * kk), 1e-24))
        km = k * (1.0 + (a - 1.0) * kaw_ref[...])
        kk_ref[:, ts] = kk
        bb_ref[:, ts] = kk * a
        km_ref[:, ts] = km
        bonus_ref[:, ts] = head_total(r_ref[:, ts] * km * rkw_ref[...])

    def emit(b, t, y_parts):
        ys = [_lane_group_sum(p) for p in y_parts]
        mu = (jnp.sum(ys[0], axis=0, keepdims=True) + jnp.sum(ys[1], axis=0, keepdims=True)) * (1.0 / RW_HEAD)
        ds = [y - mu for y in ys]
        var = (jnp.sum(ds[0] * ds[0], axis=0, keepdims=True)
               + jnp.sum(ds[1] * ds[1], axis=0, keepdims=True)) * (1.0 / RW_HEAD)
        inv = lax.rsqrt(var + RW_LN_EPS)
        for rows, d in zip(halves, ds):
            y_ref[b, t, rows, :] = (d * inv * lnw_ref[rows, :] + lnb_ref[rows, :]
                                    + bonus_ref[b, t] * v_ref[b, t, rows, :])

    def update(b, h, t, t_next, sa):
        st = state[b][h]
        vv = v_ref[b, t, halves[h], :]
        y_acc = jnp.zeros((nh, LANES), F32)
        sa_acc = jnp.zeros((nh, LANES), F32)
        for g in range(ng):
            sg = st[g] * w_ref[b, t, g:g + 1, :] + sa * bb_ref[b, t, g:g + 1, :] + vv * km_ref[b, t, g:g + 1, :]
            st[g] = sg
            y_acc = y_acc + sg * r_ref[b, t, g:g + 1, :]
            sa_acc = sa_acc - sg * kk_ref[b, t_next, g:g + 1, :]
        return sa_acc, y_acc

    def first_dot(b, h):
        acc = jnp.zeros((nh, LANES), F32)
        for g in range(ng):
            acc = acc - state[b][h][g] * kk_ref[b, 0, g:g + 1, :]
        return acc

    def step(t, carry):
        sa00, sa01, sa10_parts, sa11_parts, y10_parts, y11_parts = carry
        t_next = jnp.minimum(t + 1, tt - 1)
        sa10 = _lane_group_sum(sa10_parts)
        sa11 = _lane_group_sum(sa11_parts)
        emit(1, jnp.maximum(t - 1, 0), (y10_parts, y11_parts))
        sa00_parts, y00_parts = update(0, 0, t, t_next, sa00)
        sa01_parts, y01_parts = update(0, 1, t, t_next, sa01)
        sa00 = _lane_group_sum(sa00_parts)
        sa10_parts, y10_parts = update(1, 0, t, t_next, sa10)
        sa01 = _lane_group_sum(sa01_parts)
        emit(0, t, (y00_parts, y01_parts))
        sa11_parts, y11_parts = update(1, 1, t, t_next, sa11)
        return sa00, sa01, sa10_parts, sa11_parts, y10_parts, y11_parts

    zero = jnp.zeros((nh, LANES), F32)
    init = (_lane_group_sum(first_dot(0, 0)), _lane_group_sum(first_dot(0, 1)),
            first_dot(1, 0), first_dot(1, 1), zero, zero)
    last = lax.fori_loop(0, tt, step, init)
    emit(1, tt - 1, last[4:])

    @pl.when(tb == nblk - 1)
    def _():
        for b in range(2):
            for h in range(2):
                sout_ref[b, :, halves[h], :] = state[b][h][...]


def _rw_scan(r, w, k, a, v, kkw, kaw, rkw, lnw, lnb, s0, *, tt):
    b, t, ng, _ = r.shape
    nblk = t // tt
    kblk = pl.BlockSpec((2, tt, ng, LANES), lambda gi, ti: (gi, ti, 0, 0))
    vblk = pl.BlockSpec((2, tt, RW_HEAD, LANES), lambda gi, ti: (gi, ti, 0, 0))
    kconst = pl.BlockSpec((ng, LANES), lambda gi, ti: (0, 0))
    vconst = pl.BlockSpec((RW_HEAD, LANES), lambda gi, ti: (0, 0))
    sblk = pl.BlockSpec((2, ng, RW_HEAD, LANES), lambda gi, ti: (gi, 0, 0, 0))
    return pl.pallas_call(
        functools.partial(_rw_scan_kernel, tt=tt, nblk=nblk),
        grid=(b // 2, nblk),
        in_specs=[kblk, kblk, kblk, kblk, vblk, kconst, kconst, kconst, vconst, vconst, sblk],
        out_specs=[vblk, sblk],
        out_shape=[jax.ShapeDtypeStruct((b, t, RW_HEAD, LANES), F32),
                   jax.ShapeDtypeStruct((b, ng, RW_HEAD, LANES), F32)],
        scratch_shapes=([pltpu.VMEM((ng, RW_HEAD // 2, LANES), F32)] * 4 + [pltpu.VMEM((2, tt, ng, LANES), F32)] * 3
                        + [pltpu.VMEM((2, tt, 1, LANES), F32)]),
        compiler_params=_cp("parallel", "arbitrary"),
        name="rw_scan",
    )(r, w, k, a, v, kkw, kaw, rkw, lnw, lnb, s0)


def _head_minor(vec):
    return vec.reshape(RW_HEADS, RW_HEAD).T.reshape(1, D_MODEL)


def _head_minor_matrix():
    new = jnp.arange(D_MODEL)
    old = (new % RW_HEADS) * RW_HEAD + new // RW_HEADS
    return (jnp.arange(D_MODEL)[:, None] == old[None, :]).astype(BF16)[None]


def _value_tiles(vec):
    lead = vec.shape[:-1]
    x = vec.reshape(*lead, RW_HEAD, 1, RW_HEADS)
    return jnp.broadcast_to(x, (*lead, RW_HEAD, LANES // RW_HEADS, RW_HEADS)).reshape(*lead, RW_HEAD, LANES)


def _state_to_scan(s):
    bsz = s.shape[0]
    s = s.reshape(bsz, RW_HEADS, RW_HEAD, RW_HEAD // 4, 4).transpose(0, 3, 2, 4, 1)
    return s.reshape(bsz, RW_HEAD // 4, RW_HEAD, LANES)


def _state_from_scan(s):
    bsz = s.shape[0]
    s = s.reshape(bsz, RW_HEAD // 4, RW_HEAD, 4, RW_HEADS).transpose(0, 4, 2, 1, 3)
    return s.reshape(bsz, RW_HEADS, RW_HEAD, RW_HEAD)


def _prep_weights(p):
    w = {}
    w["mx_w_in"] = p["mx_w_in"]
    gpad = GLA_GATE_PAD - GLA_GATE_RANK
    w["mx_w_low"] = jnp.pad(p["mx_w_in"][:, :, EVEN_MAIN_WIDTH:], ((0, 0), (0, 0), (0, gpad))).astype(BF16)
    w["gla_w_a2"] = jnp.pad(p["gla_w_a2"], ((0, 0), (0, gpad), (0, 0))).astype(BF16)
    w["mx_w_out"] = p["mx_w_out"].astype(BF16)
    n_odd = p["rw_w_rkv"].shape[0]
    perm = _head_minor_matrix()
    permute = lambda rows: _mm(rows, perm, 0, D_MODEL, tm=min(rows.shape[0], 1024), tn=1024, out_dtype=BF16)
    w["rw_w_rkv"] = permute(p["rw_w_rkv"].reshape(-1, D_MODEL)).reshape(-1, D_MODEL, D_MODEL)
    pad_c = lambda a: jnp.pad(a, ((0, 0), (0, 0), (0, RW_RANK_PAD - a.shape[2]))).astype(BF16)
    pad_r = lambda a: jnp.pad(a, ((0, 0), (0, RW_RANK_PAD - a.shape[1]), (0, 0)))
    second = jnp.concatenate([pad_r(p["rw_w2"]), pad_r(p["rw_a2"]), p["rw_g2"]], axis=1)
    second = permute(second.reshape(-1, D_MODEL)).reshape(n_odd, -1, D_MODEL)
    w["rw_w1"], w["rw_a1"], w["rw_g1"] = pad_c(p["rw_w1"]), pad_c(p["rw_a1"]), p["rw_g1"].astype(BF16)
    w["rw_w2"] = second[:, :RW_RANK_PAD]
    w["rw_a2"] = second[:, RW_RANK_PAD:2 * RW_RANK_PAD]
    w["rw_g2"] = second[:, 2 * RW_RANK_PAD:]
    w["rw_w_o"] = (p["rw_w_o"].reshape(n_odd, RW_HEADS, RW_HEAD, D_MODEL).transpose(0, 2, 1, 3)
                   .reshape(n_odd, D_MODEL, D_MODEL).astype(BF16))
    for name in ("rw_w0", "rw_a0", "rw_k_k", "rw_k_a"):
        w[name] = [_head_minor(p[name][j]) for j in range(n_odd)]
    w["rw_r_k"] = [_head_minor(p["rw_r_k"][j].reshape(-1)) for j in range(n_odd)]
    for name in ("rw_ln_w", "rw_ln_b"):
        w[name] = [_value_tiles(_head_minor(p[name][j])[0]) for j in range(n_odd)]
    w["ca_w_q"] = p["ca_w_q"].astype(BF16)
    w["ca_w_kv"] = p["ca_w_kv"]
    w["ca_w_o"] = p["ca_w_o"].astype(BF16)
    w["ffn_w_up"] = p["ffn_w_up"]
    w["ffn_conv_b"] = p["ffn_conv_b"][:, None, :]
    w["ffn_down"] = p["ffn_w_down"].astype(BF16)
    for name in ("sb_bias", "gla_b_a", "gla_norm_w", "rw_mix", "ffn_conv_w", "ln_w", "ln_b"):
        w[name] = p[name]
    return w


def _trunk(x, mem_k, mem_v, mem_cols, sb_past, page_table, gla_s0, rw_s0, rw_shift0, conv0, w, *, prompt):
    bsz, seq, _ = x.shape
    m = bsz * seq
    tm = 512 if prompt else m
    x = x.reshape(m, D_MODEL)
    new_k, new_v, new_gla, new_rw, new_shift, new_conv = [], [], [], [], [], []
    for i in range(DEPTH):
        j = i // 2
        lw, lb = w["ln_w"][i], w["ln_b"][i]
        if i % 2 == 0:
            proj = _mm(x, w["mx_w_in"], j, EVEN_MAIN_WIDTH, tm=min(m, 1024), tn=768)
            low = _mm(x, w["mx_w_low"], j, GLA_GATE_PAD, tm=min(m, 1024), tn=GLA_GATE_PAD)
            ka = proj[:, SB_WIDTH:2 * SB_WIDTH].reshape(bsz, seq, SB_HEADS, SB_HEAD_DIM)
            va = proj[:, 2 * SB_WIDTH:3 * SB_WIDTH].reshape(bsz, seq, SB_HEADS, SB_HEAD_DIM)
            new_k.append(ka)
            new_v.append(va)
            if prompt:
                oa = _sb_prompt(proj, w["sb_bias"][j], batch=bsz, seq=seq, tq=512, tk=256)
            else:
                cache_k, cache_v = sb_past
                n_phys = cache_k.shape[1]
                rows = PAGE_SIZE * SB_HEADS
                qa = proj[:, :SB_WIDTH].reshape(bsz, seq, SB_HEADS, SB_HEAD_DIM).transpose(0, 2, 1, 3)
                qa = (qa.reshape(bsz, SB_HEADS * seq, SB_HEAD_DIM) * (SB_HEAD_DIM ** -0.5)).astype(BF16)
                qa = jnp.pad(qa, ((0, 0), (0, LANES - SB_HEADS * seq), (0, 0)))
                bias_cols = jnp.pad(jnp.repeat(w["sb_bias"][j], seq), (0, LANES - SB_HEADS * seq)).reshape(1, LANES)
                pad_new = lambda a: jnp.pad(a.reshape(bsz, seq * SB_HEADS, SB_HEAD_DIM),
                                            ((0, 0), (0, rows - seq * SB_HEADS), (0, 0)))
                oa = _sb_sample(qa, bias_cols, pad_new(ka), pad_new(va),
                                cache_k[j].reshape(n_phys, rows, SB_HEAD_DIM),
                                cache_v[j].reshape(n_phys, rows, SB_HEAD_DIM), page_table, pages=8)
                oa = oa[:, :SB_HEADS * seq].reshape(bsz, SB_HEADS, seq, SB_HEAD_DIM).transpose(0, 2, 1, 3)
                oa = oa.reshape(m, SB_WIDTH).astype(BF16)
            ob, s_t = _gla(proj, low, w["gla_w_a2"][j], w["gla_b_a"][j], w["gla_norm_w"][j],
                           gla_s0[j].transpose(0, 1, 3, 2), batch=bsz, seq=seq, rows=min(seq, 256))
            new_gla.append(s_t.transpose(0, 1, 3, 2))
            o = jnp.concatenate([oa, ob.astype(BF16)], axis=-1)
            x = _mm_ln(o, w["mx_w_out"][j], x, lw[0], lb[0], tm=tm, tk=o.shape[1])
        else:
            x3 = x.reshape(bsz, seq, D_MODEL)
            new_shift.append(x3[:, -1])
            x_prev = jnp.concatenate([rw_shift0[j][:, None, :], x3[:, :-1]], axis=1).reshape(m, D_MODEL)
            xr, xw, xk, xv, xa, xg = _rw_mix(x, x_prev, w["rw_mix"][j], tm=tm)
            r, k, v = (_mm(xi, w["rw_w_rkv"], 3 * j + n, D_MODEL, tm=min(m, 1024), tn=1024)
                       for n, xi in enumerate((xr, xk, xv)))
            dec, a, gate = _rw_lora(xw, xa, xg, w["rw_w1"][j], w["rw_w2"][j], w["rw_a1"][j], w["rw_a2"][j],
                                    w["rw_g1"][j], w["rw_g2"][j], w["rw_w0"][j], w["rw_a0"][j], tm=min(m, 256))
            rows = lambda z: z.reshape(bsz, seq, D_MODEL // LANES, LANES)
            krow = lambda z: z.reshape(D_MODEL // LANES, LANES)
            y, s_new = _rw_scan(rows(r), rows(dec), rows(k), rows(a), _value_tiles(v.reshape(bsz, seq, D_MODEL)),
                                krow(w["rw_k_k"][j]), krow(w["rw_k_a"][j]), krow(w["rw_r_k"][j]),
                                w["rw_ln_w"][j], w["rw_ln_b"][j],
                                _state_to_scan(rw_s0[j]), tt=min(seq, 32))
            new_rw.append(_state_from_scan(s_new))
            y = y[..., :RW_HEADS].reshape(m, D_MODEL)
            x = _mm_ln(y, w["rw_w_o"][j], x, lw[0], lb[0], tm=tm, tk=D_MODEL, gate=gate)
        x, xb = _cross_attn(x, mem_k[i], mem_v[i], mem_cols[0], mem_cols[1], w["ca_w_q"][i], w["ca_w_o"][i],
                            lw[1], lb[1], batch=bsz, tm=min(seq, 512))
        f2 = 2 * FFN_HIDDEN
        cw, cb = w["ffn_conv_w"][i], w["ffn_conv_b"][i]
        if prompt:
            c0 = jnp.pad(conv0[i], ((0, 0), (SUBLANES - conv0[i].shape[1], 0), (0, 0)))
            c, cs = _ffn_up(xb, w["ffn_w_up"], i, cw, cb, c0, tm=min(seq, 2048), tn=256, shift=1, seqs=bsz)
            new_conv.append(cs[:, -2:, :])
        else:
            c0 = conv0[i].transpose(1, 0, 2).reshape(1, 2 * bsz, f2)
            xt = xb.reshape(bsz, seq, D_MODEL).transpose(1, 0, 2).reshape(m, D_MODEL)
            c, cs = _ffn_up(xt, w["ffn_w_up"], i, cw, cb, c0, tm=m, tn=256, shift=bsz, seqs=1)
            c = c.reshape(seq, bsz, f2).transpose(1, 0, 2).reshape(m, f2)
            new_conv.append(cs.reshape(2, bsz, f2).transpose(1, 0, 2))
        x = _glu_ln(c, w["ffn_down"], i, x, lw[2], lb[2], tm=min(m, 256), kchunk=1408)
    return (x.reshape(bsz, seq, D_MODEL), jnp.stack(new_k), jnp.stack(new_v), jnp.stack(new_gla),
            jnp.stack(new_rw), jnp.stack(new_shift), jnp.stack(new_conv))


def kernel(x_prompt, x_sample, mem_prompt, cache_sb_k, cache_sb_v, page_table, state_gla, state_rwkv,
           state_rwkv_shift, state_ffn_conv, cache_mem_k, cache_mem_v, mx_w_in, sb_bias, gla_w_a2, gla_b_a,
           gla_norm_w, mx_w_out, rw_mix, rw_w_rkv, rw_w0, rw_w1, rw_w2, rw_a0, rw_a1, rw_a2, rw_g1, rw_g2,
           rw_k_k, rw_k_a, rw_r_k, rw_ln_w, rw_ln_b, rw_w_o, ca_w_q, ca_w_kv, ca_w_o, ffn_w_up, ffn_conv_w,
           ffn_conv_b, ffn_w_down, ln_w, ln_b):
    w = _prep_weights(dict(
        mx_w_in=mx_w_in, sb_bias=sb_bias, gla_w_a2=gla_w_a2, gla_b_a=gla_b_a, gla_norm_w=gla_norm_w,
        mx_w_out=mx_w_out, rw_mix=rw_mix, rw_w_rkv=rw_w_rkv, rw_w0=rw_w0, rw_w1=rw_w1, rw_w2=rw_w2,
        rw_a0=rw_a0, rw_a1=rw_a1, rw_a2=rw_a2, rw_g1=rw_g1, rw_g2=rw_g2, rw_k_k=rw_k_k, rw_k_a=rw_k_a,
        rw_r_k=rw_r_k, rw_ln_w=rw_ln_w, rw_ln_b=rw_ln_b, rw_w_o=rw_w_o, ca_w_q=ca_w_q, ca_w_kv=ca_w_kv,
        ca_w_o=ca_w_o, ffn_w_up=ffn_w_up, ffn_conv_w=ffn_conv_w, ffn_conv_b=ffn_conv_b,
        ffn_w_down=ffn_w_down, ln_w=ln_w, ln_b=ln_b))
    n_even = (DEPTH + 1) // 2
    n_odd = DEPTH // 2
    b, _, _ = x_prompt.shape
    db = x_sample.shape[0]

    mem_rows = mem_prompt.reshape(b * N_MEM, D_MODEL)
    mem_kv = [_mm(mem_rows, w["ca_w_kv"], i, 2 * MEM_WIDTH, tm=b * N_MEM, tn=512).reshape(b, N_MEM, 2 * MEM_WIDTH)
              for i in range(DEPTH)]
    mem_shape = (DEPTH, b, N_MEM, MEM_HEADS, MEM_HEAD_DIM)
    mem_k_p = jnp.stack([kv[..., :MEM_WIDTH] for kv in mem_kv]).reshape(mem_shape)
    mem_v_p = jnp.stack([kv[..., MEM_WIDTH:] for kv in mem_kv]).reshape(mem_shape)
    gla0 = jnp.zeros((n_even, b, GLA_HEADS, GLA_DK, GLA_DV), F32)
    rw0 = jnp.zeros((n_odd, b, RW_HEADS, RW_HEAD, RW_HEAD), F32)
    sh0 = jnp.zeros((n_odd, b, D_MODEL), F32)
    cv0 = jnp.zeros((DEPTH, b, 2, 2 * FFN_HIDDEN), F32)
    y_p, sbk_p, sbv_p, gla_p, rw_p, sh_p, conv_p = _trunk(
        x_prompt, mem_kv, mem_kv, (0, 1), None, None, gla0, rw0, sh0, cv0, w, prompt=True)

    mk = cache_mem_k.reshape(DEPTH, db, N_MEM, MEM_WIDTH)
    mv = cache_mem_v.reshape(DEPTH, db, N_MEM, MEM_WIDTH)
    y_s, sbk_s, sbv_s, gla_s, rw_s, sh_s, conv_s = _trunk(
        x_sample, mk, mv, (0, 0), (cache_sb_k, cache_sb_v), page_table, state_gla, state_rwkv,
        state_rwkv_shift, state_ffn_conv, w, prompt=False)
    return (y_p, y_s, sbk_p, sbv_p, gla_p, rw_p, sh_p, conv_p, mem_k_p, mem_v_p,
            sbk_s, sbv_s, gla_s, rw_s, sh_s, conv_s)
```

```python
import functools

import jax
import jax.numpy as jnp
from jax import lax
from jax.experimental import pallas as pl
from jax.experimental.pallas import tpu as pltpu

F32 = jnp.float32
BF16 = jnp.bfloat16

D_MODEL = 2048
DEPTH = 2
PAGE_SIZE = 128
SB_HEADS = 8
SB_HEAD_DIM = 128
SB_WIDTH = SB_HEADS * SB_HEAD_DIM
GLA_HEADS = 4
GLA_DK = 128
GLA_DV = 256
GLA_K_WIDTH = GLA_HEADS * GLA_DK
GLA_V_WIDTH = GLA_HEADS * GLA_DV
GLA_GATE_RANK = 16
GLA_GATE_PAD = 128
GLA_GATE_NORMALIZER = 16.0
GLA_CHUNK = 64
GLA_NORM_EPS = 1e-5
EVEN_MAIN_WIDTH = 3 * SB_WIDTH + 2 * GLA_K_WIDTH + 2 * GLA_V_WIDTH
RW_HEAD = 64
RW_HEADS = D_MODEL // RW_HEAD
RW_LN_EPS = 64e-5
RW_RANK_PAD = 128
N_MEM = 256
MEM_HEADS = 4
MEM_HEAD_DIM = 128
MEM_WIDTH = MEM_HEADS * MEM_HEAD_DIM
FFN_HIDDEN = 5504
FFN_PAD = 5632
LN_EPS = 1e-5
ALPHA = (2.0 * DEPTH) ** 0.25

LANES = 128
SUBLANES = 8
VMEM_LIMIT_MB = 56

NT_DIMS = (((1,), (1,)), ((), ()))
TN_DIMS = (((0,), (0,)), ((), ()))


def _row_dtype(rows):
    return BF16 if rows % (2 * SUBLANES) == 0 else F32


def _cp(*sem):
    return pltpu.CompilerParams(dimension_semantics=sem, vmem_limit_bytes=VMEM_LIMIT_MB * 1024 * 1024)


def _dot(a, b):
    return jnp.dot(a, b, preferred_element_type=F32)


def _dot_nt(a, b):
    return lax.dot_general(a, b, NT_DIMS, preferred_element_type=F32)


def _softplus_neg_abs(z):
    return jnp.log1p(jnp.exp(-jnp.abs(z)))


def _log_sigmoid(z):
    return jnp.minimum(z, 0.0) - _softplus_neg_abs(z)


def _sigmoid(z):
    return 1.0 / (1.0 + jnp.exp(-z))


def _split_bf16(x):
    hi = x.astype(BF16)
    lo = (x - hi.astype(F32)).astype(BF16)
    return hi, lo


def _layer_norm(y, w, b):
    mu = jnp.mean(y, axis=-1, keepdims=True)
    d = y - mu
    var = jnp.mean(d * d, axis=-1, keepdims=True)
    return d * lax.rsqrt(var + LN_EPS) * w + b


def _mm_kernel(x_ref, w_ref, o_ref, wb_ref):
    @pl.when(pl.program_id(1) == 0)
    def _():
        wb_ref[...] = w_ref[...].astype(BF16)

    o_ref[...] = _dot(x_ref[...].astype(BF16), wb_ref[...]).astype(o_ref.dtype)


def _mm(x, w, lead, n_out, *, tm, tn, out_dtype=F32):
    m, k = x.shape
    return pl.pallas_call(
        _mm_kernel,
        grid=(n_out // tn, m // tm),
        in_specs=[pl.BlockSpec((tm, k), lambda j, i: (i, 0)),
                  pl.BlockSpec((None, k, tn), lambda j, i: (lead, 0, j))],
        out_specs=pl.BlockSpec((tm, tn), lambda j, i: (i, j)),
        out_shape=jax.ShapeDtypeStruct((m, n_out), out_dtype),
        scratch_shapes=[pltpu.VMEM((k, tn), BF16)],
        compiler_params=_cp("parallel", "arbitrary"),
        name="mm",
    )(x, w)


def _mm_ln_kernel(*refs, nk, gated):
    if gated:
        x_ref, g_ref, w_ref, res_ref, lw_ref, lb_ref, o_ref = refs[:7]
        x = (x_ref[...] * g_ref[...]).astype(BF16)
    else:
        x_ref, w_ref, res_ref, lw_ref, lb_ref, o_ref = refs[:6]
        x = x_ref[...].astype(BF16)
    part = _dot(x, w_ref[...])

    def finish(h):
        o_ref[...] = _layer_norm(ALPHA * res_ref[...] + h, lw_ref[...], lb_ref[...])

    if nk == 1:
        finish(part)
        return
    acc_ref = refs[-1]
    kk = pl.program_id(1)

    @pl.when(kk == 0)
    def _():
        acc_ref[...] = part

    @pl.when(kk > 0)
    def _():
        acc_ref[...] += part

    @pl.when(kk == nk - 1)
    def _():
        finish(acc_ref[...])


def _mm_ln(x, w, res, lw, lb, *, tm, tk, gate=None):
    m, k = x.shape
    n = w.shape[1]
    nk = k // tk
    gated = gate is not None
    xs = pl.BlockSpec((tm, tk), lambda i, kk: (i, kk))
    row = pl.BlockSpec((tm, n), lambda i, kk: (i, 0))
    vec = pl.BlockSpec((1, n), lambda i, kk: (0, 0))
    in_specs = [xs] + ([xs] if gated else []) + [pl.BlockSpec((tk, n), lambda i, kk: (kk, 0)), row, vec, vec]
    args = [x] + ([gate] if gated else []) + [w, res, lw.reshape(1, n), lb.reshape(1, n)]
    return pl.pallas_call(
        functools.partial(_mm_ln_kernel, nk=nk, gated=gated),
        grid=(m // tm, nk),
        in_specs=in_specs,
        out_specs=row,
        out_shape=jax.ShapeDtypeStruct((m, n), F32),
        scratch_shapes=[pltpu.VMEM((tm, n), F32)] if nk > 1 else [],
        compiler_params=_cp("parallel", "arbitrary"),
        name="mm_ln",
    )(*args)


def _glu_ln_kernel(cv_ref, cg_ref, w_ref, res_ref, lw_ref, lb_ref, o_ref, *, kchunk):
    k = cv_ref.shape[1]
    h = None
    for c0 in range(0, k, kchunk):
        c1 = min(c0 + kchunk, k)
        g = cg_ref[:, c0:c1].astype(F32)
        act = (g * _sigmoid(g) * cv_ref[:, c0:c1].astype(F32)).astype(BF16)
        part = _dot(act, w_ref[c0:c1, :])
        h = part if h is None else h + part
    o_ref[...] = _layer_norm(ALPHA * res_ref[...] + h, lw_ref[...], lb_ref[...])


def _glu_ln(c, w, lead, res, lw, lb, *, tm, kchunk):
    m = c.shape[0]
    _, k, n = w.shape
    row = pl.BlockSpec((tm, n), lambda i: (i, 0))
    vec = pl.BlockSpec((1, n), lambda i: (0, 0))
    return pl.pallas_call(
        functools.partial(_glu_ln_kernel, kchunk=kchunk),
        grid=(m // tm,),
        in_specs=[pl.BlockSpec((tm, k), lambda i: (i, 0)), pl.BlockSpec((tm, k), lambda i: (i, 1)),
                  pl.BlockSpec((None, k, n), lambda i: (lead, 0, 0), pipeline_mode=pl.Buffered(1)),
                  row, vec, vec],
        out_specs=row,
        out_shape=jax.ShapeDtypeStruct((m, n), F32),
        compiler_params=_cp("parallel"),
        name="glu_ln",
    )(c, c, w, res, lw.reshape(1, n), lb.reshape(1, n))


def _ca_kernel(x_ref, wq_ref, mk_ref, mv_ref, wo_ref, lw_ref, lb_ref, o_ref, ob_ref):
    x = x_ref[...]
    q = _dot(x.astype(BF16), wq_ref[...])
    heads = []
    for h in range(MEM_HEADS):
        cols = slice(h * MEM_HEAD_DIM, (h + 1) * MEM_HEAD_DIM)
        s = _dot_nt(q[:, cols].astype(BF16), mk_ref[:, cols].astype(BF16)) * (MEM_HEAD_DIM ** -0.5)
        e = jnp.exp(s - jnp.max(s, axis=-1, keepdims=True))
        p = e / jnp.sum(e, axis=-1, keepdims=True)
        heads.append(_dot(p.astype(BF16), mv_ref[:, cols].astype(BF16)))
    o = jnp.concatenate(heads, axis=-1)
    y = _layer_norm(ALPHA * x + _dot(o.astype(BF16), wo_ref[...]), lw_ref[...], lb_ref[...])
    o_ref[...] = y
    ob_ref[...] = y.astype(ob_ref.dtype)


def _cross_attn(x, mem_k, mem_v, k_col, v_col, wq, wo, lw, lb, *, batch, tm):
    m = x.shape[0]
    tiles = m // batch // tm
    vec = pl.BlockSpec((1, D_MODEL), lambda b, i: (0, 0))
    row = pl.BlockSpec((tm, D_MODEL), lambda b, i: (b * tiles + i, 0))
    return pl.pallas_call(
        _ca_kernel,
        grid=(batch, tiles),
        in_specs=[row,
                  pl.BlockSpec((D_MODEL, MEM_WIDTH), lambda b, i: (0, 0)),
                  pl.BlockSpec((None, N_MEM, MEM_WIDTH), lambda b, i: (b, 0, k_col)),
                  pl.BlockSpec((None, N_MEM, MEM_WIDTH), lambda b, i: (b, 0, v_col)),
                  pl.BlockSpec((MEM_WIDTH, D_MODEL), lambda b, i: (0, 0)),
                  vec, vec],
        out_specs=[row, row],
        out_shape=[jax.ShapeDtypeStruct((m, D_MODEL), F32), jax.ShapeDtypeStruct((m, D_MODEL), _row_dtype(tm))],
        compiler_params=_cp("parallel", "arbitrary"),
        name="cross_attn",
    )(x, wq, mem_k, mem_v, wo, lw.reshape(1, -1), lb.reshape(1, -1))


def _ffn_up_kernel(x_ref, w_ref, cw_ref, cb_ref, c0_ref, c_ref, cs_ref, carry_ref, buf,
                   *, tm, shift, carry_rows, tiles_per_seq):
    i = pl.program_id(0)
    j = pl.program_id(1)
    cr = carry_rows

    @pl.when(i % tiles_per_seq == 0)
    def _():
        buf[0:cr, :] = c0_ref[...]

    @pl.when(i % tiles_per_seq != 0)
    def _():
        buf[0:cr, :] = carry_ref[j]

    buf[cr:cr + tm, :] = _dot(x_ref[...].astype(BF16), w_ref[...].astype(BF16))
    c = (cw_ref[0:1, :] * buf[cr - 2 * shift:cr - 2 * shift + tm, :]
         + cw_ref[1:2, :] * buf[cr - shift:cr - shift + tm, :]
         + cw_ref[2:3, :] * buf[cr:cr + tm, :] + cb_ref[...])
    c_ref[...] = c.astype(c_ref.dtype)
    last = buf[tm:tm + cr, :]
    cs_ref[...] = last
    carry_ref[j] = last


def _ffn_up(xb, w, lead, cw, cb, c0, *, tm, tn, shift, seqs):
    m, k = xb.shape
    n = w.shape[2]
    carry_rows = c0.shape[1]
    tiles_per_seq = m // seqs // tm
    col = lambda i, j: (0, j)
    c, tails = pl.pallas_call(
        functools.partial(_ffn_up_kernel, tm=tm, shift=shift, carry_rows=carry_rows, tiles_per_seq=tiles_per_seq),
        grid=(m // tm, n // tn),
        in_specs=[pl.BlockSpec((tm, k), lambda i, j: (i, 0)),
                  pl.BlockSpec((None, k, tn), lambda i, j: (lead, 0, j)),
                  pl.BlockSpec((3, tn), col), pl.BlockSpec((1, tn), col),
                  pl.BlockSpec((None, carry_rows, tn), lambda i, j: (i // tiles_per_seq, 0, j))],
        out_specs=[pl.BlockSpec((tm, tn), lambda i, j: (i, j)),
                   pl.BlockSpec((None, carry_rows, tn), lambda i, j: (i, 0, j))],
        out_shape=[jax.ShapeDtypeStruct((m, n), BF16), jax.ShapeDtypeStruct((m // tm, carry_rows, n), F32)],
        scratch_shapes=[pltpu.VMEM((n // tn, carry_rows, tn), F32), pltpu.VMEM((tm + carry_rows, tn), F32)],
        compiler_params=_cp("arbitrary", "arbitrary"),
        name="ffn_up",
    )(xb, w, cw, cb, c0)
    return c, tails[tiles_per_seq - 1::tiles_per_seq]


def _sb_weights(z, carry, tri, mask):
    sp = jnp.log(1.0 + jnp.exp(-jnp.abs(z)))
    log_beta = jnp.minimum(z, 0.0) - sp
    log_keep = jnp.minimum(-z, 0.0) - sp
    if mask is not None:
        log_keep = jnp.where(mask, log_keep, 0.0)
    hi, lo = _split_bf16(log_keep)
    a = jnp.exp(log_beta + _dot(hi, tri) + _dot(lo, tri) + carry)
    if mask is not None:
        a = jnp.where(mask, a, 0.0)
    return a, carry + jnp.sum(log_keep, axis=1, keepdims=True)


def _sb_prompt_kernel(bias_ref, q_ref, k_ref, v_ref, o_ref, *, tq, tk):
    h = pl.program_id(1)
    qi = pl.program_id(2)
    bias = bias_ref[h]
    nsub = tq // tk
    q = (q_ref[...] * (SB_HEAD_DIM ** -0.5)).astype(BF16)
    r = lax.broadcasted_iota(jnp.int32, (tk, tk), 0)
    c = lax.broadcasted_iota(jnp.int32, (tk, tk), 1)
    tri = jnp.where(r > c, 1.0, 0.0).astype(BF16)
    trow = lax.broadcasted_iota(jnp.int32, (tq, tk), 0)
    scol = lax.broadcasted_iota(jnp.int32, (tq, tk), 1)

    def block(start, acc, carry, mask):
        k = k_ref[pl.ds(start, tk), :].astype(BF16)
        v = v_ref[pl.ds(start, tk), :].astype(BF16)
        a, carry = _sb_weights(_dot_nt(q, k) + bias, carry, tri, mask)
        return acc + _dot(a.astype(BF16), v), carry

    acc = jnp.zeros((tq, SB_HEAD_DIM), F32)
    carry = jnp.zeros((tq, 1), F32)
    for d in reversed(range(nsub)):
        start = pl.multiple_of(qi * tq + d * tk, tk)
        acc, carry = block(start, acc, carry, (scol + d * tk) < trow)

    def body(it, state):
        acc, carry = state
        for d in range(nsub):
            kb = (qi - it) * nsub - 1 - d
            acc, carry = block(pl.multiple_of(kb * tk, tk), acc, carry, None)
        return acc, carry

    acc, carry = lax.fori_loop(0, qi, body, (acc, carry))
    o_ref[...] = acc.astype(o_ref.dtype)


def _sb_prompt(proj, bias, *, batch, seq, tq, tk):
    nq = seq // tq
    grid_spec = pltpu.PrefetchScalarGridSpec(
        num_scalar_prefetch=1,
        grid=(batch, SB_HEADS, nq),
        in_specs=[pl.BlockSpec((tq, SB_HEAD_DIM), lambda b, h, i, s: (b * nq + i, h)),
                  pl.BlockSpec((seq, SB_HEAD_DIM), lambda b, h, i, s: (b, SB_HEADS + h)),
                  pl.BlockSpec((seq, SB_HEAD_DIM), lambda b, h, i, s: (b, 2 * SB_HEADS + h))],
        out_specs=pl.BlockSpec((tq, SB_HEAD_DIM), lambda b, h, i, s: (b * nq + i, h)),
    )
    return pl.pallas_call(
        functools.partial(_sb_prompt_kernel, tq=tq, tk=tk),
        grid_spec=grid_spec,
        out_shape=jax.ShapeDtypeStruct((batch * seq, SB_WIDTH), BF16),
        compiler_params=_cp("parallel", "parallel", "arbitrary"),
        name="sb_prompt",
    )(bias, proj, proj, proj)


def _sb_sample_kernel(pt_ref, q_ref, bias_ref, kn_ref, vn_ref, *rest, pages):
    k_refs, v_refs = rest[:pages], rest[pages:2 * pages]
    o_ref, acc_ref, carry_ref = rest[2 * pages:]
    s = pl.program_id(1)
    n = LANES
    q = q_ref[...]
    row = lax.broadcasted_iota(jnp.int32, (n, n), 0)
    col = lax.broadcasted_iota(jnp.int32, (n, n), 1)
    tri = jnp.where(col > row, 1.0, 0.0).astype(BF16)
    col_head = col // SUBLANES
    col_t = col % SUBLANES
    bias = bias_ref[...]

    def head_rows(ref):
        return jnp.concatenate([ref[pl.ds(h, PAGE_SIZE, stride=SB_HEADS), :].astype(BF16)
                                for h in range(SB_HEADS)], axis=0)

    def attend(k_list, v_list, mask, acc, carry):
        ps = PAGE_SIZE
        zs = []
        for k_ref in k_list:
            zz = _dot_nt(head_rows(k_ref), q)
            z = zz[0:ps, :]
            for h in range(1, SB_HEADS):
                z = jnp.where(col_head == h, zz[h * ps:(h + 1) * ps, :], z)
            zs.append(z)
        z = jnp.concatenate(zs, axis=0) + bias
        sp = jnp.log(1.0 + jnp.exp(-jnp.abs(z)))
        log_beta = jnp.minimum(z, 0.0) - sp
        log_keep = jnp.minimum(-z, 0.0) - sp
        if mask is not None:
            log_keep = jnp.where(mask, log_keep, 0.0)
        hi, lo = _split_bf16(log_keep)
        np_ = len(k_list)
        rhs = jnp.concatenate([x[p * ps:(p + 1) * ps, :] for p in range(np_) for x in (hi, lo)], axis=1)
        both = _dot(tri, rhs)
        ats = []
        for p in range(np_):
            rs = slice(p * ps, (p + 1) * ps)
            tail = both[:, 2 * p * n:(2 * p + 1) * n] + both[:, (2 * p + 1) * n:(2 * p + 2) * n]
            a = jnp.exp(log_beta[rs, :] + tail + carry)
            if mask is not None:
                a = jnp.where(mask, a, 0.0)
            carry = carry + jnp.sum(log_keep[rs, :], axis=0, keepdims=True)
            ats.append(a.T.astype(BF16))
        at = jnp.concatenate(ats, axis=1)
        vs = [head_rows(v_ref) for v_ref in v_list]
        outs = []
        for h in range(SB_HEADS):
            vh = jnp.concatenate([v[h * ps:(h + 1) * ps, :] for v in vs], axis=0)
            outs.append(_dot(at[h * SUBLANES:(h + 1) * SUBLANES, :], vh))
        return acc + jnp.concatenate(outs, axis=0), carry

    @pl.when(s == 0)
    def _():
        acc0, carry0 = attend([kn_ref], [vn_ref], row < col_t,
                              jnp.zeros(acc_ref.shape, F32), jnp.zeros((1, n), F32))
        acc_ref[...] = acc0
        carry_ref[...] = carry0

    acc, carry = attend(k_refs, v_refs, None, acc_ref[...], carry_ref[...])
    acc_ref[...] = acc
    carry_ref[...] = carry

    @pl.when(s == pl.num_programs(1) - 1)
    def _():
        o_ref[...] = acc


def _sb_sample(q_rows, bias_cols, k_new, v_new, cache_k, cache_v, page_table, *, pages):
    nb, n_pages = page_table.shape
    rows = PAGE_SIZE * SB_HEADS
    out_rows = SB_HEADS * SUBLANES

    def page_spec(p):
        return pl.BlockSpec((None, rows, SB_HEAD_DIM),
                            lambda b, s, pt: (pt[b, n_pages - 1 - (s * pages + p)], 0, 0))

    per_b = lambda b, s, pt: (b, 0, 0)
    grid_spec = pltpu.PrefetchScalarGridSpec(
        num_scalar_prefetch=1,
        grid=(nb, n_pages // pages),
        in_specs=[pl.BlockSpec((None, LANES, SB_HEAD_DIM), per_b),
                  pl.BlockSpec((1, LANES), lambda b, s, pt: (0, 0)),
                  pl.BlockSpec((None, rows, SB_HEAD_DIM), per_b),
                  pl.BlockSpec((None, rows, SB_HEAD_DIM), per_b)]
        + [page_spec(p) for p in range(pages)] * 2,
        out_specs=pl.BlockSpec((None, out_rows, SB_HEAD_DIM), per_b),
        scratch_shapes=[pltpu.VMEM((out_rows, SB_HEAD_DIM), F32), pltpu.VMEM((1, LANES), F32)],
    )
    return pl.pallas_call(
        functools.partial(_sb_sample_kernel, pages=pages),
        grid_spec=grid_spec,
        out_shape=jax.ShapeDtypeStruct((nb, out_rows, SB_HEAD_DIM), F32),
        compiler_params=_cp("parallel", "arbitrary"),
        name="sb_sample",
    )(page_table, q_rows, bias_cols, k_new, v_new, *([cache_k] * pages), *([cache_v] * pages))


def _gla_kernel(q_ref, k_ref, v_ref, g_ref, low_ref, wa_ref, ba_ref, nw_ref, s0_ref, o_ref, sout_ref, st_ref,
                *, rows, nsteps):
    si = pl.program_id(1)
    c = GLA_CHUNK
    rp = max(rows, c)
    chunks = rp // c

    @pl.when(si == 0)
    def _():
        st_ref[...] = s0_ref[...]

    def pad(x):
        if rows == rp:
            return x
        return jnp.concatenate([x, jnp.zeros((rp - rows, x.shape[1]), x.dtype)], axis=0)

    r = lax.broadcasted_iota(jnp.int32, (rp, rp), 0)
    cc = lax.broadcasted_iota(jnp.int32, (rp, rp), 1)
    causal = (r >= cc) & ((r // c) == (cc // c))
    log_a = _log_sigmoid(_dot(pad(low_ref[...]).astype(BF16), wa_ref[...]) + ba_ref[...])
    log_a = log_a * (1.0 / GLA_GATE_NORMALIZER)
    if rows < rp:
        log_a = jnp.where(lax.broadcasted_iota(jnp.int32, log_a.shape, 0) < rows, log_a, 0.0)
    hi, lo = _split_bf16(log_a)
    ltri = jnp.where(causal, 1.0, 0.0).astype(BF16)
    b = _dot(ltri, hi) + _dot(ltri, lo)
    b_last = [b[(ci + 1) * c - 1:(ci + 1) * c, :] for ci in range(chunks)]
    b_end = jnp.concatenate([jnp.broadcast_to(bl, (c, bl.shape[1])) for bl in b_last], axis=0)
    k = pad(k_ref[...])
    q_dec = (pad(q_ref[...]) * (GLA_DK ** -0.5) * jnp.exp(b)).astype(BF16)
    k_inv = (k * jnp.exp(-b)).astype(BF16)
    k_state = (k * jnp.exp(b_end - b)).astype(BF16)
    v = pad(v_ref[...]).astype(BF16)
    g = pad(g_ref[...])
    outs = []
    for h in range(GLA_HEADS):
        kc = slice(h * GLA_DK, (h + 1) * GLA_DK)
        vc = slice(h * GLA_DV, (h + 1) * GLA_DV)
        scores = _dot_nt(q_dec[:, kc], k_inv[:, kc])
        o_intra = _dot(jnp.where(causal, scores, 0.0).astype(BF16), v[:, vc])
        st = st_ref[h]
        o_inter = []
        for ci in range(chunks):
            rs = slice(ci * c, (ci + 1) * c)
            o_inter.append(_dot_nt(q_dec[rs, kc], st.astype(BF16)))
            st = st * jnp.exp(b_last[ci][:, kc]) + lax.dot_general(v[rs, vc], k_state[rs, kc], TN_DIMS,
                                                                   preferred_element_type=F32)
        st_ref[h] = st
        o = o_intra + jnp.concatenate(o_inter, axis=0)
        o = o * lax.rsqrt(jnp.mean(o * o, axis=-1, keepdims=True) + GLA_NORM_EPS) * nw_ref[...]
        gh = g[:, vc]
        outs.append(o * (gh * _sigmoid(gh)))
    o_ref[...] = jnp.concatenate(outs, axis=1)[:rows, :].astype(o_ref.dtype)

    @pl.when(si == nsteps - 1)
    def _():
        sout_ref[...] = st_ref[...]


def _gla(proj, low, wa, ba, nw, s0t, *, batch, seq, rows):
    nsteps = seq // rows
    qc = 3 * SB_WIDTH // GLA_K_WIDTH
    vc = (3 * SB_WIDTH + 2 * GLA_K_WIDTH) // GLA_V_WIDTH
    rowblk = lambda b, i: b * nsteps + i
    st = pl.BlockSpec((None, GLA_HEADS, GLA_DV, GLA_DK), lambda b, i: (b, 0, 0, 0))
    return pl.pallas_call(
        functools.partial(_gla_kernel, rows=rows, nsteps=nsteps),
        grid=(batch, nsteps),
        in_specs=[pl.BlockSpec((rows, GLA_K_WIDTH), lambda b, i: (rowblk(b, i), qc)),
                  pl.BlockSpec((rows, GLA_K_WIDTH), lambda b, i: (rowblk(b, i), qc + 1)),
                  pl.BlockSpec((rows, GLA_V_WIDTH), lambda b, i: (rowblk(b, i), vc)),
                  pl.BlockSpec((rows, GLA_V_WIDTH), lambda b, i: (rowblk(b, i), vc + 1)),
                  pl.BlockSpec((rows, GLA_GATE_PAD), lambda b, i: (rowblk(b, i), 0)),
                  pl.BlockSpec((GLA_GATE_PAD, GLA_K_WIDTH), lambda b, i: (0, 0)),
                  pl.BlockSpec((1, GLA_K_WIDTH), lambda b, i: (0, 0)),
                  pl.BlockSpec((1, GLA_DV), lambda b, i: (0, 0)),
                  st],
        out_specs=[pl.BlockSpec((rows, GLA_V_WIDTH), lambda b, i: (rowblk(b, i), 0)), st],
        out_shape=[jax.ShapeDtypeStruct((batch * seq, GLA_V_WIDTH), _row_dtype(rows)),
                   jax.ShapeDtypeStruct((batch, GLA_HEADS, GLA_DV, GLA_DK), F32)],
        scratch_shapes=[pltpu.VMEM((GLA_HEADS, GLA_DV, GLA_DK), F32)],
        compiler_params=_cp("parallel", "arbitrary"),
        name="gla",
    )(proj, proj, proj, proj, low, wa, ba.reshape(1, -1), nw.reshape(1, -1), s0t)


def _rw_mix_kernel(x_ref, xp_ref, mix_ref, *o_refs):
    x = x_ref[...]
    xx = xp_ref[...] - x
    for m, o_ref in enumerate(o_refs):
        o_ref[...] = (x + xx * mix_ref[m:m + 1, :]).astype(o_ref.dtype)


def _rw_mix(x, x_prev, mix, *, tm):
    m, d = x.shape
    row = pl.BlockSpec((tm, d), lambda i: (i, 0))
    return pl.pallas_call(
        _rw_mix_kernel,
        grid=(m // tm,),
        in_specs=[row, row, pl.BlockSpec((6, d), lambda i: (0, 0))],
        out_specs=[row] * 6,
        out_shape=[jax.ShapeDtypeStruct((m, d), BF16)] * 6,
        compiler_params=_cp("parallel"),
        name="rw_mix",
    )(x, x_prev, mix)


def _store_lane_rows(o_ref, val):
    rows, width = val.shape
    n = width // LANES
    for g in range(n):
        o_ref[pl.ds(g, rows, stride=n), :] = val[:, g * LANES:(g + 1) * LANES]


def _rw_lora_kernel(xw_ref, xa_ref, xg_ref, w1_ref, w2_ref, a1_ref, a2_ref, g1_ref, g2_ref, w0_ref, a0_ref,
                    dec_ref, a_ref, g_ref):
    hw = jnp.tanh(_dot(xw_ref[...], w1_ref[...]))
    wl = w0_ref[...] + _dot(hw.astype(BF16), w2_ref[...])
    w_log = _log_sigmoid(wl) - 0.5
    _store_lane_rows(dec_ref, jnp.exp(-jnp.exp(w_log)))
    ha = _dot(xa_ref[...], a1_ref[...])
    _store_lane_rows(a_ref, _sigmoid(a0_ref[...] + _dot(ha.astype(BF16), a2_ref[...])))
    hg = _sigmoid(_dot(xg_ref[...], g1_ref[...]))
    g_ref[...] = _dot(hg.astype(BF16), g2_ref[...])


def _rw_lora(xw, xa, xg, w1, w2, a1, a2, g1, g2, w0, a0, *, tm):
    m, d = xw.shape
    ng = d // LANES
    row = pl.BlockSpec((tm, d), lambda i: (i, 0))
    lane_rows = pl.BlockSpec((tm * ng, LANES), lambda i: (i, 0))
    full = lambda a: pl.BlockSpec(a.shape, lambda i: (0, 0))
    w0 = w0.reshape(1, d)
    a0 = a0.reshape(1, d)
    consts = [w1, w2, a1, a2, g1, g2, w0, a0]
    return pl.pallas_call(
        _rw_lora_kernel,
        grid=(m // tm,),
        in_specs=[row, row, row] + [full(a) for a in consts],
        out_specs=[lane_rows, lane_rows, row],
        out_shape=[jax.ShapeDtypeStruct((m * ng, LANES), F32)] * 2 + [jax.ShapeDtypeStruct((m, d), F32)],
        compiler_params=_cp("parallel"),
        name="rw_lora",
    )(xw, xa, xg, *consts)


def _mm_lane_rows_kernel(x_ref, w_ref, o_ref):
    _store_lane_rows(o_ref, _dot(x_ref[...], w_ref[...]))


def _mm_lane_rows(x, w, lead, *, tm):
    m, k = x.shape
    n = w.shape[2]
    ng = n // LANES
    return pl.pallas_call(
        _mm_lane_rows_kernel,
        grid=(m // tm,),
        in_specs=[pl.BlockSpec((tm, k), lambda i: (i, 0)),
                  pl.BlockSpec((None, k, n), lambda i: (lead, 0, 0))],
        out_specs=pl.BlockSpec((tm * ng, LANES), lambda i: (i, 0)),
        out_shape=jax.ShapeDtypeStruct((m * ng, LANES), F32),
        compiler_params=_cp("parallel"),
        name="mm_lane_rows",
    )(x, w)


def _lane_group_sum(x):
    axis = x.ndim - 1
    x = x + pltpu.roll(x, RW_HEADS, axis=axis)
    return x + pltpu.roll(x, 2 * RW_HEADS, axis=axis)


def _rw_scan_kernel(r_ref, w_ref, k_ref, a_ref, v_ref, kkw_ref, kaw_ref, rkw_ref, lnw_ref, lnb_ref, s0_ref,
                    y_ref, sout_ref, s00_ref, s01_ref, s10_ref, s11_ref, kk_ref, bb_ref, km_ref, bonus_ref,
                    *, tt, nblk):
    tb = pl.program_id(1)
    ng = D_MODEL // LANES
    nh = RW_HEAD // 2
    halves = (slice(0, nh), slice(nh, RW_HEAD))
    state = ((s00_ref, s01_ref), (s10_ref, s11_ref))

    @pl.when(tb == 0)
    def _():
        for b in range(2):
            for h in range(2):
                state[b][h][...] = s0_ref[b, :, halves[h], :]

    def head_total(x):
        return _lane_group_sum(jnp.sum(x, axis=2, keepdims=True))

    for c0 in range(0, tt, SUBLANES):
        ts = slice(c0, min(c0 + SUBLANES, tt))
        k = k_ref[:, ts]
        a = a_ref[:, ts]
        kk = k * kkw_ref[...]
        kk = kk * lax.rsqrt(jnp.maximum(head_total(kk * kk), 1e-24))
        km = k * (1.0 + (a - 1.0) * kaw_ref[...])
        kk_ref[:, ts] = kk
        bb_ref[:, ts] = kk * a
        km_ref[:, ts] = km
        bonus_ref[:, ts] = head_total(r_ref[:, ts] * km * rkw_ref[...])

    def emit(b, t, y_parts):
        ys = [_lane_group_sum(p) for p in y_parts]
        mu = (jnp.sum(ys[0], axis=0, keepdims=True) + jnp.sum(ys[1], axis=0, keepdims=True)) * (1.0 / RW_HEAD)
        ds = [y - mu for y in ys]
        var = (jnp.sum(ds[0] * ds[0], axis=0, keepdims=True)
               + jnp.sum(ds[1] * ds[1], axis=0, keepdims=True)) * (1.0 / RW_HEAD)
        inv = lax.rsqrt(var + RW_LN_EPS)
        for rows, d in zip(halves, ds):
            y_ref[b, t, rows, :] = (d * inv * lnw_ref[rows, :] + lnb_ref[rows, :]
                                    + bonus_ref[b, t] * v_ref[b, t, rows, :])

    def update(b, h, t, t_next, sa):
        st = state[b][h]
        vv = v_ref[b, t, halves[h], :]
        y_acc = jnp.zeros((nh, LANES), F32)
        sa_acc = jnp.zeros((nh, LANES), F32)
        for g in range(ng):
            sg = st[g] * w_ref[b, t, g:g + 1, :] + sa * bb_ref[b, t, g:g + 1, :] + vv * km_ref[b, t, g:g + 1, :]
            st[g] = sg
            y_acc = y_acc + sg * r_ref[b, t, g:g + 1, :]
            sa_acc = sa_acc - sg * kk_ref[b, t_next, g:g + 1, :]
        return sa_acc, y_acc

    def first_dot(b, h):
        acc = jnp.zeros((nh, LANES), F32)
        for g in range(ng):
            acc = acc - state[b][h][g] * kk_ref[b, 0, g:g + 1, :]
        return acc

    def step(t, carry):
        sa00, sa01, sa10_parts, sa11_parts, y10_parts, y11_parts = carry
        t_next = jnp.minimum(t + 1, tt - 1)
        sa10 = _lane_group_sum(sa10_parts)
        sa11 = _lane_group_sum(sa11_parts)
        emit(1, jnp.maximum(t - 1, 0), (y10_parts, y11_parts))
        sa00_parts, y00_parts = update(0, 0, t, t_next, sa00)
        sa01_parts, y01_parts = update(0, 1, t, t_next, sa01)
        sa00 = _lane_group_sum(sa00_parts)
        sa10_parts, y10_parts = update(1, 0, t, t_next, sa10)
        sa01 = _lane_group_sum(sa01_parts)
        emit(0, t, (y00_parts, y01_parts))
        sa11_parts, y11_parts = update(1, 1, t, t_next, sa11)
        return sa00, sa01, sa10_parts, sa11_parts, y10_parts, y11_parts

    zero = jnp.zeros((nh, LANES), F32)
    init = (_lane_group_sum(first_dot(0, 0)), _lane_group_sum(first_dot(0, 1)),
            first_dot(1, 0), first_dot(1, 1), zero, zero)
    last = lax.fori_loop(0, tt, step, init)
    emit(1, tt - 1, last[4:])

    @pl.when(tb == nblk - 1)
    def _():
        for b in range(2):
            for h in range(2):
                sout_ref[b, :, halves[h], :] = state[b][h][...]


def _rw_scan(r, w, k, a, v, kkw, kaw, rkw, lnw, lnb, s0, *, tt):
    b, t, ng, _ = r.shape
    nblk = t // tt
    kblk = pl.BlockSpec((2, tt, ng, LANES), lambda gi, ti: (gi, ti, 0, 0))
    vblk = pl.BlockSpec((2, tt, RW_HEAD, LANES), lambda gi, ti: (gi, ti, 0, 0))
    kconst = pl.BlockSpec((ng, LANES), lambda gi, ti: (0, 0))
    vconst = pl.BlockSpec((RW_HEAD, LANES), lambda gi, ti: (0, 0))
    sblk = pl.BlockSpec((2, ng, RW_HEAD, LANES), lambda gi, ti: (gi, 0, 0, 0))
    return pl.pallas_call(
        functools.partial(_rw_scan_kernel, tt=tt, nblk=nblk),
        grid=(b // 2, nblk),
        in_specs=[kblk, kblk, kblk, kblk, vblk, kconst, kconst, kconst, vconst, vconst, sblk],
        out_specs=[vblk, sblk],
        out_shape=[jax.ShapeDtypeStruct((b, t, RW_HEAD, LANES), F32),
                   jax.ShapeDtypeStruct((b, ng, RW_HEAD, LANES), F32)],
        scratch_shapes=([pltpu.VMEM((ng, RW_HEAD // 2, LANES), F32)] * 4 + [pltpu.VMEM((2, tt, ng, LANES), F32)] * 3
                        + [pltpu.VMEM((2, tt, 1, LANES), F32)]),
        compiler_params=_cp("parallel", "arbitrary"),
        name="rw_scan",
    )(r, w, k, a, v, kkw, kaw, rkw, lnw, lnb, s0)


def _head_minor(vec):
    return vec.reshape(RW_HEADS, RW_HEAD).T.reshape(1, D_MODEL)


def _head_minor_matrix():
    new = jnp.arange(D_MODEL)
    old = (new % RW_HEADS) * RW_HEAD + new // RW_HEADS
    return (jnp.arange(D_MODEL)[:, None] == old[None, :]).astype(BF16)[None]


def _value_tiles(vec):
    lead = vec.shape[:-1]
    x = vec.reshape(*lead, RW_HEAD, 1, RW_HEADS)
    return jnp.broadcast_to(x, (*lead, RW_HEAD, LANES // RW_HEADS, RW_HEADS)).reshape(*lead, RW_HEAD, LANES)


def _state_to_scan(s):
    bsz = s.shape[0]
    s = s.reshape(bsz, RW_HEADS, RW_HEAD, RW_HEAD // 4, 4).transpose(0, 3, 2, 4, 1)
    return s.reshape(bsz, RW_HEAD // 4, RW_HEAD, LANES)


def _state_from_scan(s):
    bsz = s.shape[0]
    s = s.reshape(bsz, RW_HEAD // 4, RW_HEAD, 4, RW_HEADS).transpose(0, 4, 2, 1, 3)
    return s.reshape(bsz, RW_HEADS, RW_HEAD, RW_HEAD)


def _prep_weights(p):
    w = {}
    w["mx_w_in"] = p["mx_w_in"]
    gpad = GLA_GATE_PAD - GLA_GATE_RANK
    w["mx_w_low"] = jnp.pad(p["mx_w_in"][:, :, EVEN_MAIN_WIDTH:], ((0, 0), (0, 0), (0, gpad))).astype(BF16)
    w["gla_w_a2"] = jnp.pad(p["gla_w_a2"], ((0, 0), (0, gpad), (0, 0))).astype(BF16)
    w["mx_w_out"] = p["mx_w_out"].astype(BF16)
    n_odd = p["rw_w_rkv"].shape[0]
    perm = _head_minor_matrix()
    permute = lambda rows: _mm(rows, perm, 0, D_MODEL, tm=min(rows.shape[0], 1024), tn=1024, out_dtype=BF16)
    w["rw_w_rkv"] = permute(p["rw_w_rkv"].reshape(-1, D_MODEL)).reshape(-1, D_MODEL, D_MODEL)
    pad_c = lambda a: jnp.pad(a, ((0, 0), (0, 0), (0, RW_RANK_PAD - a.shape[2]))).astype(BF16)
    pad_r = lambda a: jnp.pad(a, ((0, 0), (0, RW_RANK_PAD - a.shape[1]), (0, 0)))
    second = jnp.concatenate([pad_r(p["rw_w2"]), pad_r(p["rw_a2"]), p["rw_g2"]], axis=1)
    second = permute(second.reshape(-1, D_MODEL)).reshape(n_odd, -1, D_MODEL)
    w["rw_w1"], w["rw_a1"], w["rw_g1"] = pad_c(p["rw_w1"]), pad_c(p["rw_a1"]), p["rw_g1"].astype(BF16)
    w["rw_w2"] = second[:, :RW_RANK_PAD]
    w["rw_a2"] = second[:, RW_RANK_PAD:2 * RW_RANK_PAD]
    w["rw_g2"] = second[:, 2 * RW_RANK_PAD:]
    w["rw_w_o"] = (p["rw_w_o"].reshape(n_odd, RW_HEADS, RW_HEAD, D_MODEL).transpose(0, 2, 1, 3)
                   .reshape(n_odd, D_MODEL, D_MODEL).astype(BF16))
    for name in ("rw_w0", "rw_a0", "rw_k_k", "rw_k_a"):
        w[name] = [_head_minor(p[name][j]) for j in range(n_odd)]
    w["rw_r_k"] = [_head_minor(p["rw_r_k"][j].reshape(-1)) for j in range(n_odd)]
    for name in ("rw_ln_w", "rw_ln_b"):
        w[name] = [_value_tiles(_head_minor(p[name][j])[0]) for j in range(n_odd)]
    w["ca_w_q"] = p["ca_w_q"].astype(BF16)
    w["ca_w_kv"] = p["ca_w_kv"]
    w["ca_w_o"] = p["ca_w_o"].astype(BF16)
    w["ffn_w_up"] = p["ffn_w_up"]
    w["ffn_conv_b"] = p["ffn_conv_b"][:, None, :]
    w["ffn_down"] = p["ffn_w_down"].astype(BF16)
    for name in ("sb_bias", "gla_b_a", "gla_norm_w", "rw_mix", "ffn_conv_w", "ln_w", "ln_b"):
        w[name] = p[name]
    return w


def _trunk(x, mem_k, mem_v, mem_cols, sb_past, page_table, gla_s0, rw_s0, rw_shift0, conv0, w, *, prompt):
    bsz, seq, _ = x.shape
    m = bsz * seq
    tm = 512 if prompt else m
    x = x.reshape(m, D_MODEL)
    new_k, new_v, new_gla, new_rw, new_shift, new_conv = [], [], [], [], [], []
    for i in range(DEPTH):
        j = i // 2
        lw, lb = w["ln_w"][i], w["ln_b"][i]
        if i % 2 == 0:
            proj = _mm(x, w["mx_w_in"], j, EVEN_MAIN_WIDTH, tm=min(m, 1024), tn=768)
            low = _mm(x, w["mx_w_low"], j, GLA_GATE_PAD, tm=min(m, 1024), tn=GLA_GATE_PAD)
            ka = proj[:, SB_WIDTH:2 * SB_WIDTH].reshape(bsz, seq, SB_HEADS, SB_HEAD_DIM)
            va = proj[:, 2 * SB_WIDTH:3 * SB_WIDTH].reshape(bsz, seq, SB_HEADS, SB_HEAD_DIM)
            new_k.append(ka)
            new_v.append(va)
            if prompt:
                oa = _sb_prompt(proj, w["sb_bias"][j], batch=bsz, seq=seq, tq=512, tk=256)
            else:
                cache_k, cache_v = sb_past
                n_phys = cache_k.shape[1]
                rows = PAGE_SIZE * SB_HEADS
                qa = proj[:, :SB_WIDTH].reshape(bsz, seq, SB_HEADS, SB_HEAD_DIM).transpose(0, 2, 1, 3)
                qa = (qa.reshape(bsz, SB_HEADS * seq, SB_HEAD_DIM) * (SB_HEAD_DIM ** -0.5)).astype(BF16)
                qa = jnp.pad(qa, ((0, 0), (0, LANES - SB_HEADS * seq), (0, 0)))
                bias_cols = jnp.pad(jnp.repeat(w["sb_bias"][j], seq), (0, LANES - SB_HEADS * seq)).reshape(1, LANES)
                pad_new = lambda a: jnp.pad(a.reshape(bsz, seq * SB_HEADS, SB_HEAD_DIM),
                                            ((0, 0), (0, rows - seq * SB_HEADS), (0, 0)))
                oa = _sb_sample(qa, bias_cols, pad_new(ka), pad_new(va),
                                cache_k[j].reshape(n_phys, rows, SB_HEAD_DIM),
                                cache_v[j].reshape(n_phys, rows, SB_HEAD_DIM), page_table, pages=8)
                oa = oa[:, :SB_HEADS * seq].reshape(bsz, SB_HEADS, seq, SB_HEAD_DIM).transpose(0, 2, 1, 3)
                oa = oa.reshape(m, SB_WIDTH).astype(BF16)
            ob, s_t = _gla(proj, low, w["gla_w_a2"][j], w["gla_b_a"][j], w["gla_norm_w"][j],
                           gla_s0[j].transpose(0, 1, 3, 2), batch=bsz, seq=seq, rows=min(seq, 256))
            new_gla.append(s_t.transpose(0, 1, 3, 2))
            o = jnp.concatenate([oa, ob.astype(BF16)], axis=-1)
            x = _mm_ln(o, w["mx_w_out"][j], x, lw[0], lb[0], tm=tm, tk=o.shape[1])
        else:
            x3 = x.reshape(bsz, seq, D_MODEL)
            new_shift.append(x3[:, -1])
            x_prev = jnp.concatenate([rw_shift0[j][:, None, :], x3[:, :-1]], axis=1).reshape(m, D_MODEL)
            xr, xw, xk, xv, xa, xg = _rw_mix(x, x_prev, w["rw_mix"][j], tm=tm)
            r = _mm_lane_rows(xr, w["rw_w_rkv"], 3 * j, tm=min(m, 512))
            k = _mm_lane_rows(xk, w["rw_w_rkv"], 3 * j + 1, tm=min(m, 512))
            v = _mm(xv, w["rw_w_rkv"], 3 * j + 2, D_MODEL, tm=min(m, 1024), tn=1024)
            dec, a, gate = _rw_lora(xw, xa, xg, w["rw_w1"][j], w["rw_w2"][j], w["rw_a1"][j], w["rw_a2"][j],
                                    w["rw_g1"][j], w["rw_g2"][j], w["rw_w0"][j], w["rw_a0"][j], tm=min(m, 256))
            rows = lambda z: z.reshape(bsz, seq, D_MODEL // LANES, LANES)
            krow = lambda z: z.reshape(D_MODEL // LANES, LANES)
            y, s_new = _rw_scan(rows(r), rows(dec), rows(k), rows(a), _value_tiles(v.reshape(bsz, seq, D_MODEL)),
                                krow(w["rw_k_k"][j]), krow(w["rw_k_a"][j]), krow(w["rw_r_k"][j]),
                                w["rw_ln_w"][j], w["rw_ln_b"][j],
                                _state_to_scan(rw_s0[j]), tt=min(seq, 32))
            new_rw.append(_state_from_scan(s_new))
            y = y[..., :RW_HEADS].reshape(m, D_MODEL)
            x = _mm_ln(y, w["rw_w_o"][j], x, lw[0], lb[0], tm=tm, tk=D_MODEL, gate=gate)
        x, xb = _cross_attn(x, mem_k[i], mem_v[i], mem_cols[0], mem_cols[1], w["ca_w_q"][i], w["ca_w_o"][i],
                            lw[1], lb[1], batch=bsz, tm=min(seq, 512))
        f2 = 2 * FFN_HIDDEN
        cw, cb = w["ffn_conv_w"][i], w["ffn_conv_b"][i]
        if prompt:
            c0 = jnp.pad(conv0[i], ((0, 0), (SUBLANES - conv0[i].shape[1], 0), (0, 0)))
            c, cs = _ffn_up(xb, w["ffn_w_up"], i, cw, cb, c0, tm=min(seq, 2048), tn=256, shift=1, seqs=bsz)
            new_conv.append(cs[:, -2:, :])
        else:
            c0 = conv0[i].transpose(1, 0, 2).reshape(1, 2 * bsz, f2)
            xt = xb.reshape(bsz, seq, D_MODEL).transpose(1, 0, 2).reshape(m, D_MODEL)
            c, cs = _ffn_up(xt, w["ffn_w_up"], i, cw, cb, c0, tm=m, tn=256, shift=bsz, seqs=1)
            c = c.reshape(seq, bsz, f2).transpose(1, 0, 2).reshape(m, f2)
            new_conv.append(cs.reshape(2, bsz, f2).transpose(1, 0, 2))
        x = _glu_ln(c, w["ffn_down"], i, x, lw[2], lb[2], tm=min(m, 256), kchunk=1408)
    return (x.reshape(bsz, seq, D_MODEL), jnp.stack(new_k), jnp.stack(new_v), jnp.stack(new_gla),
            jnp.stack(new_rw), jnp.stack(new_shift), jnp.stack(new_conv))


def kernel(x_prompt, x_sample, mem_prompt, cache_sb_k, cache_sb_v, page_table, state_gla, state_rwkv,
           state_rwkv_shift, state_ffn_conv, cache_mem_k, cache_mem_v, mx_w_in, sb_bias, gla_w_a2, gla_b_a,
           gla_norm_w, mx_w_out, rw_mix, rw_w_rkv, rw_w0, rw_w1, rw_w2, rw_a0, rw_a1, rw_a2, rw_g1, rw_g2,
           rw_k_k, rw_k_a, rw_r_k, rw_ln_w, rw_ln_b, rw_w_o, ca_w_q, ca_w_kv, ca_w_o, ffn_w_up, ffn_conv_w,
           ffn_conv_b, ffn_w_down, ln_w, ln_b):
    w = _prep_weights(dict(
        mx_w_in=mx_w_in, sb_bias=sb_bias, gla_w_a2=gla_w_a2, gla_b_a=gla_b_a, gla_norm_w=gla_norm_w,
        mx_w_out=mx_w_out, rw_mix=rw_mix, rw_w_rkv=rw_w_rkv, rw_w0=rw_w0, rw_w1=rw_w1, rw_w2=rw_w2,
        rw_a0=rw_a0, rw_a1=rw_a1, rw_a2=rw_a2, rw_g1=rw_g1, rw_g2=rw_g2, rw_k_k=rw_k_k, rw_k_a=rw_k_a,
        rw_r_k=rw_r_k, rw_ln_w=rw_ln_w, rw_ln_b=rw_ln_b, rw_w_o=rw_w_o, ca_w_q=ca_w_q, ca_w_kv=ca_w_kv,
        ca_w_o=ca_w_o, ffn_w_up=ffn_w_up, ffn_conv_w=ffn_conv_w, ffn_conv_b=ffn_conv_b,
        ffn_w_down=ffn_w_down, ln_w=ln_w, ln_b=ln_b))
    n_even = (DEPTH + 1) // 2
    n_odd = DEPTH // 2
    b, _, _ = x_prompt.shape
    db = x_sample.shape[0]

    mem_rows = mem_prompt.reshape(b * N_MEM, D_MODEL)
    mem_kv = [_mm(mem_rows, w["ca_w_kv"], i, 2 * MEM_WIDTH, tm=b * N_MEM, tn=512).reshape(b, N_MEM, 2 * MEM_WIDTH)
              for i in range(DEPTH)]
    mem_shape = (DEPTH, b, N_MEM, MEM_HEADS, MEM_HEAD_DIM)
    mem_k_p = jnp.stack([kv[..., :MEM_WIDTH] for kv in mem_kv]).reshape(mem_shape)
    mem_v_p = jnp.stack([kv[..., MEM_WIDTH:] for kv in mem_kv]).reshape(mem_shape)
    gla0 = jnp.zeros((n_even, b, GLA_HEADS, GLA_DK, GLA_DV), F32)
    rw0 = jnp.zeros((n_odd, b, RW_HEADS, RW_HEAD, RW_HEAD), F32)
    sh0 = jnp.zeros((n_odd, b, D_MODEL), F32)
    cv0 = jnp.zeros((DEPTH, b, 2, 2 * FFN_HIDDEN), F32)
    y_p, sbk_p, sbv_p, gla_p, rw_p, sh_p, conv_p = _trunk(
        x_prompt, mem_kv, mem_kv, (0, 1), None, None, gla0, rw0, sh0, cv0, w, prompt=True)

    mk = cache_mem_k.reshape(DEPTH, db, N_MEM, MEM_WIDTH)
    mv = cache_mem_v.reshape(DEPTH, db, N_MEM, MEM_WIDTH)
    y_s, sbk_s, sbv_s, gla_s, rw_s, sh_s, conv_s = _trunk(
        x_sample, mk, mv, (0, 0), (cache_sb_k, cache_sb_v), page_table, state_gla, state_rwkv,
        state_rwkv_shift, state_ffn_conv, w, prompt=False)
    return (y_p, y_s, sbk_p, sbv_p, gla_p, rw_p, sh_p, conv_p, mem_k_p, mem_v_p,
            sbk_s, sbv_s, gla_s, rw_s, sh_s, conv_s)
```

```python
import functools

import jax
import jax.numpy as jnp
from jax import lax
from jax.experimental import pallas as pl
from jax.experimental.pallas import tpu as pltpu

F32 = jnp.float32
BF16 = jnp.bfloat16

D_MODEL = 2048
DEPTH = 2
PAGE_SIZE = 128
SB_HEADS = 8
SB_HEAD_DIM = 128
SB_WIDTH = SB_HEADS * SB_HEAD_DIM
GLA_HEADS = 4
GLA_DK = 128
GLA_DV = 256
GLA_K_WIDTH = GLA_HEADS * GLA_DK
GLA_V_WIDTH = GLA_HEADS * GLA_DV
GLA_GATE_RANK = 16
GLA_GATE_PAD = 128
GLA_GATE_NORMALIZER = 16.0
GLA_CHUNK = 64
GLA_NORM_EPS = 1e-5
EVEN_MAIN_WIDTH = 3 * SB_WIDTH + 2 * GLA_K_WIDTH + 2 * GLA_V_WIDTH
RW_HEAD = 64
RW_HEADS = D_MODEL // RW_HEAD
RW_LN_EPS = 64e-5
RW_RANK_PAD = 128
N_MEM = 256
MEM_HEADS = 4
MEM_HEAD_DIM = 128
MEM_WIDTH = MEM_HEADS * MEM_HEAD_DIM
FFN_HIDDEN = 5504
FFN_PAD = 5632
LN_EPS = 1e-5
ALPHA = (2.0 * DEPTH) ** 0.25

LANES = 128
SUBLANES = 8
VMEM_LIMIT_MB = 56

NT_DIMS = (((1,), (1,)), ((), ()))
TN_DIMS = (((0,), (0,)), ((), ()))


def _row_dtype(rows):
    return BF16 if rows % (2 * SUBLANES) == 0 else F32


def _cp(*sem):
    return pltpu.CompilerParams(dimension_semantics=sem, vmem_limit_bytes=VMEM_LIMIT_MB * 1024 * 1024)


def _dot(a, b):
    return jnp.dot(a, b, preferred_element_type=F32)


def _dot_nt(a, b):
    return lax.dot_general(a, b, NT_DIMS, preferred_element_type=F32)


def _softplus_neg_abs(z):
    return jnp.log1p(jnp.exp(-jnp.abs(z)))


def _log_sigmoid(z):
    return jnp.minimum(z, 0.0) - _softplus_neg_abs(z)


def _sigmoid(z):
    return 1.0 / (1.0 + jnp.exp(-z))


def _split_bf16(x):
    hi = x.astype(BF16)
    lo = (x - hi.astype(F32)).astype(BF16)
    return hi, lo


def _layer_norm(y, w, b):
    mu = jnp.mean(y, axis=-1, keepdims=True)
    d = y - mu
    var = jnp.mean(d * d, axis=-1, keepdims=True)
    return d * lax.rsqrt(var + LN_EPS) * w + b


def _mm_kernel(x_ref, w_ref, o_ref, wb_ref):
    @pl.when(pl.program_id(1) == 0)
    def _():
        wb_ref[...] = w_ref[...].astype(BF16)

    o_ref[...] = _dot(x_ref[...].astype(BF16), wb_ref[...]).astype(o_ref.dtype)


def _mm(x, w, lead, n_out, *, tm, tn, out_dtype=F32):
    m, k = x.shape
    return pl.pallas_call(
        _mm_kernel,
        grid=(n_out // tn, m // tm),
        in_specs=[pl.BlockSpec((tm, k), lambda j, i: (i, 0)),
                  pl.BlockSpec((None, k, tn), lambda j, i: (lead, 0, j))],
        out_specs=pl.BlockSpec((tm, tn), lambda j, i: (i, j)),
        out_shape=jax.ShapeDtypeStruct((m, n_out), out_dtype),
        scratch_shapes=[pltpu.VMEM((k, tn), BF16)],
        compiler_params=_cp("parallel", "arbitrary"),
        name="mm",
    )(x, w)


def _mm_ln_kernel(*refs, nk, gated):
    if gated:
        x_ref, g_ref, w_ref, res_ref, lw_ref, lb_ref, o_ref = refs[:7]
        x = (x_ref[...] * g_ref[...]).astype(BF16)
    else:
        x_ref, w_ref, res_ref, lw_ref, lb_ref, o_ref = refs[:6]
        x = x_ref[...].astype(BF16)
    part = _dot(x, w_ref[...])

    def finish(h):
        o_ref[...] = _layer_norm(ALPHA * res_ref[...] + h, lw_ref[...], lb_ref[...])

    if nk == 1:
        finish(part)
        return
    acc_ref = refs[-1]
    kk = pl.program_id(1)

    @pl.when(kk == 0)
    def _():
        acc_ref[...] = part

    @pl.when(kk > 0)
    def _():
        acc_ref[...] += part

    @pl.when(kk == nk - 1)
    def _():
        finish(acc_ref[...])


def _mm_ln(x, w, res, lw, lb, *, tm, tk, gate=None):
    m, k = x.shape
    n = w.shape[1]
    nk = k // tk
    gated = gate is not None
    xs = pl.BlockSpec((tm, tk), lambda i, kk: (i, kk))
    row = pl.BlockSpec((tm, n), lambda i, kk: (i, 0))
    vec = pl.BlockSpec((1, n), lambda i, kk: (0, 0))
    in_specs = [xs] + ([xs] if gated else []) + [pl.BlockSpec((tk, n), lambda i, kk: (kk, 0)), row, vec, vec]
    args = [x] + ([gate] if gated else []) + [w, res, lw.reshape(1, n), lb.reshape(1, n)]
    return pl.pallas_call(
        functools.partial(_mm_ln_kernel, nk=nk, gated=gated),
        grid=(m // tm, nk),
        in_specs=in_specs,
        out_specs=row,
        out_shape=jax.ShapeDtypeStruct((m, n), F32),
        scratch_shapes=[pltpu.VMEM((tm, n), F32)] if nk > 1 else [],
        compiler_params=_cp("parallel", "arbitrary"),
        name="mm_ln",
    )(*args)


def _glu_ln_kernel(cv_ref, cg_ref, w_ref, res_ref, lw_ref, lb_ref, o_ref, *, kchunk):
    k = cv_ref.shape[1]
    h = None
    for c0 in range(0, k, kchunk):
        c1 = min(c0 + kchunk, k)
        g = cg_ref[:, c0:c1].astype(F32)
        act = (g * _sigmoid(g) * cv_ref[:, c0:c1].astype(F32)).astype(BF16)
        part = _dot(act, w_ref[c0:c1, :])
        h = part if h is None else h + part
    o_ref[...] = _layer_norm(ALPHA * res_ref[...] + h, lw_ref[...], lb_ref[...])


def _glu_ln(c, w, lead, res, lw, lb, *, tm, kchunk):
    m = c.shape[0]
    _, k, n = w.shape
    row = pl.BlockSpec((tm, n), lambda i: (i, 0))
    vec = pl.BlockSpec((1, n), lambda i: (0, 0))
    return pl.pallas_call(
        functools.partial(_glu_ln_kernel, kchunk=kchunk),
        grid=(m // tm,),
        in_specs=[pl.BlockSpec((tm, k), lambda i: (i, 0)), pl.BlockSpec((tm, k), lambda i: (i, 1)),
                  pl.BlockSpec((None, k, n), lambda i: (lead, 0, 0), pipeline_mode=pl.Buffered(1)),
                  row, vec, vec],
        out_specs=row,
        out_shape=jax.ShapeDtypeStruct((m, n), F32),
        compiler_params=_cp("parallel"),
        name="glu_ln",
    )(c, c, w, res, lw.reshape(1, n), lb.reshape(1, n))


def _ca_kernel(x_ref, wq_ref, mk_ref, mv_ref, wo_ref, lw_ref, lb_ref, o_ref, ob_ref):
    x = x_ref[...]
    q = _dot(x.astype(BF16), wq_ref[...])
    heads = []
    for h in range(MEM_HEADS):
        cols = slice(h * MEM_HEAD_DIM, (h + 1) * MEM_HEAD_DIM)
        s = _dot_nt(q[:, cols].astype(BF16), mk_ref[:, cols].astype(BF16)) * (MEM_HEAD_DIM ** -0.5)
        e = jnp.exp(s - jnp.max(s, axis=-1, keepdims=True))
        p = e / jnp.sum(e, axis=-1, keepdims=True)
        heads.append(_dot(p.astype(BF16), mv_ref[:, cols].astype(BF16)))
    o = jnp.concatenate(heads, axis=-1)
    y = _layer_norm(ALPHA * x + _dot(o.astype(BF16), wo_ref[...]), lw_ref[...], lb_ref[...])
    o_ref[...] = y
    ob_ref[...] = y.astype(ob_ref.dtype)


def _cross_attn(x, mem_k, mem_v, k_col, v_col, wq, wo, lw, lb, *, batch, tm):
    m = x.shape[0]
    tiles = m // batch // tm
    vec = pl.BlockSpec((1, D_MODEL), lambda b, i: (0, 0))
    row = pl.BlockSpec((tm, D_MODEL), lambda b, i: (b * tiles + i, 0))
    return pl.pallas_call(
        _ca_kernel,
        grid=(batch, tiles),
        in_specs=[row,
                  pl.BlockSpec((D_MODEL, MEM_WIDTH), lambda b, i: (0, 0)),
                  pl.BlockSpec((None, N_MEM, MEM_WIDTH), lambda b, i: (b, 0, k_col)),
                  pl.BlockSpec((None, N_MEM, MEM_WIDTH), lambda b, i: (b, 0, v_col)),
                  pl.BlockSpec((MEM_WIDTH, D_MODEL), lambda b, i: (0, 0)),
                  vec, vec],
        out_specs=[row, row],
        out_shape=[jax.ShapeDtypeStruct((m, D_MODEL), F32), jax.ShapeDtypeStruct((m, D_MODEL), _row_dtype(tm))],
        compiler_params=_cp("parallel", "arbitrary"),
        name="cross_attn",
    )(x, wq, mem_k, mem_v, wo, lw.reshape(1, -1), lb.reshape(1, -1))


def _ffn_up_kernel(x_ref, w_ref, cw_ref, cb_ref, c0_ref, c_ref, cs_ref, carry_ref, buf,
                   *, tm, shift, carry_rows, tiles_per_seq):
    i = pl.program_id(0)
    j = pl.program_id(1)
    cr = carry_rows

    @pl.when(i % tiles_per_seq == 0)
    def _():
        buf[0:cr, :] = c0_ref[...]

    @pl.when(i % tiles_per_seq != 0)
    def _():
        buf[0:cr, :] = carry_ref[j]

    buf[cr:cr + tm, :] = _dot(x_ref[...].astype(BF16), w_ref[...].astype(BF16))
    c = (cw_ref[0:1, :] * buf[cr - 2 * shift:cr - 2 * shift + tm, :]
         + cw_ref[1:2, :] * buf[cr - shift:cr - shift + tm, :]
         + cw_ref[2:3, :] * buf[cr:cr + tm, :] + cb_ref[...])
    c_ref[...] = c.astype(c_ref.dtype)
    last = buf[tm:tm + cr, :]
    cs_ref[...] = last
    carry_ref[j] = last


def _ffn_up(xb, w, lead, cw, cb, c0, *, tm, tn, shift, seqs):
    m, k = xb.shape
    n = w.shape[2]
    carry_rows = c0.shape[1]
    tiles_per_seq = m // seqs // tm
    col = lambda i, j: (0, j)
    c, tails = pl.pallas_call(
        functools.partial(_ffn_up_kernel, tm=tm, shift=shift, carry_rows=carry_rows, tiles_per_seq=tiles_per_seq),
        grid=(m // tm, n // tn),
        in_specs=[pl.BlockSpec((tm, k), lambda i, j: (i, 0)),
                  pl.BlockSpec((None, k, tn), lambda i, j: (lead, 0, j)),
                  pl.BlockSpec((3, tn), col), pl.BlockSpec((1, tn), col),
                  pl.BlockSpec((None, carry_rows, tn), lambda i, j: (i // tiles_per_seq, 0, j))],
        out_specs=[pl.BlockSpec((tm, tn), lambda i, j: (i, j)),
                   pl.BlockSpec((None, carry_rows, tn), lambda i, j: (i, 0, j))],
        out_shape=[jax.ShapeDtypeStruct((m, n), BF16), jax.ShapeDtypeStruct((m // tm, carry_rows, n), F32)],
        scratch_shapes=[pltpu.VMEM((n // tn, carry_rows, tn), F32), pltpu.VMEM((tm + carry_rows, tn), F32)],
        compiler_params=_cp("arbitrary", "arbitrary"),
        name="ffn_up",
    )(xb, w, cw, cb, c0)
    return c, tails[tiles_per_seq - 1::tiles_per_seq]


def _sb_weights(z, carry, tri, mask):
    sp = jnp.log(1.0 + jnp.exp(-jnp.abs(z)))
    log_beta = jnp.minimum(z, 0.0) - sp
    log_keep = jnp.minimum(-z, 0.0) - sp
    if mask is not None:
        log_keep = jnp.where(mask, log_keep, 0.0)
    hi, lo = _split_bf16(log_keep)
    a = jnp.exp(log_beta + _dot(hi, tri) + _dot(lo, tri) + carry)
    if mask is not None:
        a = jnp.where(mask, a, 0.0)
    return a, carry + jnp.sum(log_keep, axis=1, keepdims=True)


def _sb_prompt_kernel(bias_ref, q_ref, k_ref, v_ref, o_ref, *, tq, tk):
    h = pl.program_id(1)
    qi = pl.program_id(2)
    bias = bias_ref[h]
    nsub = tq // tk
    q = (q_ref[...] * (SB_HEAD_DIM ** -0.5)).astype(BF16)
    r = lax.broadcasted_iota(jnp.int32, (tk, tk), 0)
    c = lax.broadcasted_iota(jnp.int32, (tk, tk), 1)
    tri = jnp.where(r > c, 1.0, 0.0).astype(BF16)
    trow = lax.broadcasted_iota(jnp.int32, (tq, tk), 0)
    scol = lax.broadcasted_iota(jnp.int32, (tq, tk), 1)

    def block(start, acc, carry, mask):
        k = k_ref[pl.ds(start, tk), :].astype(BF16)
        v = v_ref[pl.ds(start, tk), :].astype(BF16)
        a, carry = _sb_weights(_dot_nt(q, k) + bias, carry, tri, mask)
        return acc + _dot(a.astype(BF16), v), carry

    acc = jnp.zeros((tq, SB_HEAD_DIM), F32)
    carry = jnp.zeros((tq, 1), F32)
    for d in reversed(range(nsub)):
        start = pl.multiple_of(qi * tq + d * tk, tk)
        acc, carry = block(start, acc, carry, (scol + d * tk) < trow)

    def body(it, state):
        acc, carry = state
        for d in range(nsub):
            kb = (qi - it) * nsub - 1 - d
            acc, carry = block(pl.multiple_of(kb * tk, tk), acc, carry, None)
        return acc, carry

    acc, carry = lax.fori_loop(0, qi, body, (acc, carry))
    o_ref[...] = acc.astype(o_ref.dtype)


def _sb_prompt(proj, bias, *, batch, seq, tq, tk):
    nq = seq // tq
    grid_spec = pltpu.PrefetchScalarGridSpec(
        num_scalar_prefetch=1,
        grid=(batch, SB_HEADS, nq),
        in_specs=[pl.BlockSpec((tq, SB_HEAD_DIM), lambda b, h, i, s: (b * nq + i, h)),
                  pl.BlockSpec((seq, SB_HEAD_DIM), lambda b, h, i, s: (b, SB_HEADS + h)),
                  pl.BlockSpec((seq, SB_HEAD_DIM), lambda b, h, i, s: (b, 2 * SB_HEADS + h))],
        out_specs=pl.BlockSpec((tq, SB_HEAD_DIM), lambda b, h, i, s: (b * nq + i, h)),
    )
    return pl.pallas_call(
        functools.partial(_sb_prompt_kernel, tq=tq, tk=tk),
        grid_spec=grid_spec,
        out_shape=jax.ShapeDtypeStruct((batch * seq, SB_WIDTH), BF16),
        compiler_params=_cp("parallel", "parallel", "arbitrary"),
        name="sb_prompt",
    )(bias, proj, proj, proj)


def _sb_sample_kernel(pt_ref, q_ref, bias_ref, kn_ref, vn_ref, *rest, pages):
    k_refs, v_refs = rest[:pages], rest[pages:2 * pages]
    o_ref, acc_ref, carry_ref = rest[2 * pages:]
    s = pl.program_id(1)
    n = LANES
    q = q_ref[...]
    row = lax.broadcasted_iota(jnp.int32, (n, n), 0)
    col = lax.broadcasted_iota(jnp.int32, (n, n), 1)
    tri = jnp.where(col > row, 1.0, 0.0).astype(BF16)
    col_head = col // SUBLANES
    col_t = col % SUBLANES
    bias = bias_ref[...]

    def head_rows(ref):
        return jnp.concatenate([ref[pl.ds(h, PAGE_SIZE, stride=SB_HEADS), :].astype(BF16)
                                for h in range(SB_HEADS)], axis=0)

    def attend(k_list, v_list, mask, acc, carry):
        ps = PAGE_SIZE
        zs = []
        for k_ref in k_list:
            zz = _dot_nt(head_rows(k_ref), q)
            z = zz[0:ps, :]
            for h in range(1, SB_HEADS):
                z = jnp.where(col_head == h, zz[h * ps:(h + 1) * ps, :], z)
            zs.append(z)
        z = jnp.concatenate(zs, axis=0) + bias
        sp = jnp.log(1.0 + jnp.exp(-jnp.abs(z)))
        log_beta = jnp.minimum(z, 0.0) - sp
        log_keep = jnp.minimum(-z, 0.0) - sp
        if mask is not None:
            log_keep = jnp.where(mask, log_keep, 0.0)
        hi, lo = _split_bf16(log_keep)
        np_ = len(k_list)
        rhs = jnp.concatenate([x[p * ps:(p + 1) * ps, :] for p in range(np_) for x in (hi, lo)], axis=1)
        both = _dot(tri, rhs)
        ats = []
        for p in range(np_):
            rs = slice(p * ps, (p + 1) * ps)
            tail = both[:, 2 * p * n:(2 * p + 1) * n] + both[:, (2 * p + 1) * n:(2 * p + 2) * n]
            a = jnp.exp(log_beta[rs, :] + tail + carry)
            if mask is not None:
                a = jnp.where(mask, a, 0.0)
            carry = carry + jnp.sum(log_keep[rs, :], axis=0, keepdims=True)
            ats.append(a.T.astype(BF16))
        at = jnp.concatenate(ats, axis=1)
        vs = [head_rows(v_ref) for v_ref in v_list]
        outs = []
        for h in range(SB_HEADS):
            vh = jnp.concatenate([v[h * ps:(h + 1) * ps, :] for v in vs], axis=0)
            outs.append(_dot(at[h * SUBLANES:(h + 1) * SUBLANES, :], vh))
        return acc + jnp.concatenate(outs, axis=0), carry

    @pl.when(s == 0)
    def _():
        acc0, carry0 = attend([kn_ref], [vn_ref], row < col_t,
                              jnp.zeros(acc_ref.shape, F32), jnp.zeros((1, n), F32))
        acc_ref[...] = acc0
        carry_ref[...] = carry0

    acc, carry = attend(k_refs, v_refs, None, acc_ref[...], carry_ref[...])
    acc_ref[...] = acc
    carry_ref[...] = carry

    @pl.when(s == pl.num_programs(1) - 1)
    def _():
        o_ref[...] = acc


def _sb_sample(q_rows, bias_cols, k_new, v_new, cache_k, cache_v, page_table, *, pages):
    nb, n_pages = page_table.shape
    rows = PAGE_SIZE * SB_HEADS
    out_rows = SB_HEADS * SUBLANES

    def page_spec(p):
        return pl.BlockSpec((None, rows, SB_HEAD_DIM),
                            lambda b, s, pt: (pt[b, n_pages - 1 - (s * pages + p)], 0, 0))

    per_b = lambda b, s, pt: (b, 0, 0)
    grid_spec = pltpu.PrefetchScalarGridSpec(
        num_scalar_prefetch=1,
        grid=(nb, n_pages // pages),
        in_specs=[pl.BlockSpec((None, LANES, SB_HEAD_DIM), per_b),
                  pl.BlockSpec((1, LANES), lambda b, s, pt: (0, 0)),
                  pl.BlockSpec((None, rows, SB_HEAD_DIM), per_b),
                  pl.BlockSpec((None, rows, SB_HEAD_DIM), per_b)]
        + [page_spec(p) for p in range(pages)] * 2,
        out_specs=pl.BlockSpec((None, out_rows, SB_HEAD_DIM), per_b),
        scratch_shapes=[pltpu.VMEM((out_rows, SB_HEAD_DIM), F32), pltpu.VMEM((1, LANES), F32)],
    )
    return pl.pallas_call(
        functools.partial(_sb_sample_kernel, pages=pages),
        grid_spec=grid_spec,
        out_shape=jax.ShapeDtypeStruct((nb, out_rows, SB_HEAD_DIM), F32),
        compiler_params=_cp("parallel", "arbitrary"),
        name="sb_sample",
    )(page_table, q_rows, bias_cols, k_new, v_new, *([cache_k] * pages), *([cache_v] * pages))


def _gla_kernel(q_ref, k_ref, v_ref, g_ref, low_ref, wa_ref, ba_ref, nw_ref, s0_ref, o_ref, sout_ref, st_ref,
                *, rows, nsteps):
    si = pl.program_id(1)
    c = GLA_CHUNK
    rp = max(rows, c)
    chunks = rp // c

    @pl.when(si == 0)
    def _():
        st_ref[...] = s0_ref[...]

    def pad(x):
        if rows == rp:
            return x
        return jnp.concatenate([x, jnp.zeros((rp - rows, x.shape[1]), x.dtype)], axis=0)

    r = lax.broadcasted_iota(jnp.int32, (rp, rp), 0)
    cc = lax.broadcasted_iota(jnp.int32, (rp, rp), 1)
    causal = (r >= cc) & ((r // c) == (cc // c))
    log_a = _log_sigmoid(_dot(pad(low_ref[...]).astype(BF16), wa_ref[...]) + ba_ref[...])
    log_a = log_a * (1.0 / GLA_GATE_NORMALIZER)
    if rows < rp:
        log_a = jnp.where(lax.broadcasted_iota(jnp.int32, log_a.shape, 0) < rows, log_a, 0.0)
    hi, lo = _split_bf16(log_a)
    ltri = jnp.where(causal, 1.0, 0.0).astype(BF16)
    b = _dot(ltri, hi) + _dot(ltri, lo)
    b_last = [b[(ci + 1) * c - 1:(ci + 1) * c, :] for ci in range(chunks)]
    b_end = jnp.concatenate([jnp.broadcast_to(bl, (c, bl.shape[1])) for bl in b_last], axis=0)
    k = pad(k_ref[...])
    q_dec = (pad(q_ref[...]) * (GLA_DK ** -0.5) * jnp.exp(b)).astype(BF16)
    k_inv = (k * jnp.exp(-b)).astype(BF16)
    k_state = (k * jnp.exp(b_end - b)).astype(BF16)
    v = pad(v_ref[...]).astype(BF16)
    g = pad(g_ref[...])
    outs = []
    for h in range(GLA_HEADS):
        kc = slice(h * GLA_DK, (h + 1) * GLA_DK)
        vc = slice(h * GLA_DV, (h + 1) * GLA_DV)
        scores = _dot_nt(q_dec[:, kc], k_inv[:, kc])
        o_intra = _dot(jnp.where(causal, scores, 0.0).astype(BF16), v[:, vc])
        st = st_ref[h]
        o_inter = []
        for ci in range(chunks):
            rs = slice(ci * c, (ci + 1) * c)
            o_inter.append(_dot_nt(q_dec[rs, kc], st.astype(BF16)))
            st = st * jnp.exp(b_last[ci][:, kc]) + lax.dot_general(v[rs, vc], k_state[rs, kc], TN_DIMS,
                                                                   preferred_element_type=F32)
        st_ref[h] = st
        o = o_intra + jnp.concatenate(o_inter, axis=0)
        o = o * lax.rsqrt(jnp.mean(o * o, axis=-1, keepdims=True) + GLA_NORM_EPS) * nw_ref[...]
        gh = g[:, vc]
        outs.append(o * (gh * _sigmoid(gh)))
    o_ref[...] = jnp.concatenate(outs, axis=1)[:rows, :].astype(o_ref.dtype)

    @pl.when(si == nsteps - 1)
    def _():
        sout_ref[...] = st_ref[...]


def _gla(proj, low, wa, ba, nw, s0t, *, batch, seq, rows):
    nsteps = seq // rows
    qc = 3 * SB_WIDTH // GLA_K_WIDTH
    vc = (3 * SB_WIDTH + 2 * GLA_K_WIDTH) // GLA_V_WIDTH
    rowblk = lambda b, i: b * nsteps + i
    st = pl.BlockSpec((None, GLA_HEADS, GLA_DV, GLA_DK), lambda b, i: (b, 0, 0, 0))
    return pl.pallas_call(
        functools.partial(_gla_kernel, rows=rows, nsteps=nsteps),
        grid=(batch, nsteps),
        in_specs=[pl.BlockSpec((rows, GLA_K_WIDTH), lambda b, i: (rowblk(b, i), qc)),
                  pl.BlockSpec((rows, GLA_K_WIDTH), lambda b, i: (rowblk(b, i), qc + 1)),
                  pl.BlockSpec((rows, GLA_V_WIDTH), lambda b, i: (rowblk(b, i), vc)),
                  pl.BlockSpec((rows, GLA_V_WIDTH), lambda b, i: (rowblk(b, i), vc + 1)),
                  pl.BlockSpec((rows, GLA_GATE_PAD), lambda b, i: (rowblk(b, i), 0)),
                  pl.BlockSpec((GLA_GATE_PAD, GLA_K_WIDTH), lambda b, i: (0, 0)),
                  pl.BlockSpec((1, GLA_K_WIDTH), lambda b, i: (0, 0)),
                  pl.BlockSpec((1, GLA_DV), lambda b, i: (0, 0)),
                  st],
        out_specs=[pl.BlockSpec((rows, GLA_V_WIDTH), lambda b, i: (rowblk(b, i), 0)), st],
        out_shape=[jax.ShapeDtypeStruct((batch * seq, GLA_V_WIDTH), _row_dtype(rows)),
                   jax.ShapeDtypeStruct((batch, GLA_HEADS, GLA_DV, GLA_DK), F32)],
        scratch_shapes=[pltpu.VMEM((GLA_HEADS, GLA_DV, GLA_DK), F32)],
        compiler_params=_cp("parallel", "arbitrary"),
        name="gla",
    )(proj, proj, proj, proj, low, wa, ba.reshape(1, -1), nw.reshape(1, -1), s0t)


def _rw_mix_kernel(x_ref, xp_ref, mix_ref, *o_refs):
    x = x_ref[...]
    xx = xp_ref[...] - x
    for m, o_ref in enumerate(o_refs):
        o_ref[...] = (x + xx * mix_ref[m:m + 1, :]).astype(o_ref.dtype)


def _rw_mix(x, x_prev, mix, *, tm):
    m, d = x.shape
    row = pl.BlockSpec((tm, d), lambda i: (i, 0))
    return pl.pallas_call(
        _rw_mix_kernel,
        grid=(m // tm,),
        in_specs=[row, row, pl.BlockSpec((6, d), lambda i: (0, 0))],
        out_specs=[row] * 6,
        out_shape=[jax.ShapeDtypeStruct((m, d), BF16)] * 6,
        compiler_params=_cp("parallel"),
        name="rw_mix",
    )(x, x_prev, mix)


def _store_lane_rows(o_ref, val):
    rows, width = val.shape
    n = width // LANES
    for g in range(n):
        o_ref[pl.ds(g, rows, stride=n), :] = val[:, g * LANES:(g + 1) * LANES]


def _rw_lora_kernel(xw_ref, xa_ref, xg_ref, w1_ref, w2_ref, a1_ref, a2_ref, g1_ref, g2_ref, w0_ref, a0_ref,
                    dec_ref, a_ref, g_ref):
    hw = jnp.tanh(_dot(xw_ref[...], w1_ref[...]))
    wl = w0_ref[...] + _dot(hw.astype(BF16), w2_ref[...])
    w_log = _log_sigmoid(wl) - 0.5
    _store_lane_rows(dec_ref, jnp.exp(-jnp.exp(w_log)))
    ha = _dot(xa_ref[...], a1_ref[...])
    _store_lane_rows(a_ref, _sigmoid(a0_ref[...] + _dot(ha.astype(BF16), a2_ref[...])))
    hg = _sigmoid(_dot(xg_ref[...], g1_ref[...]))
    g_ref[...] = _dot(hg.astype(BF16), g2_ref[...])


def _rw_lora(xw, xa, xg, w1, w2, a1, a2, g1, g2, w0, a0, *, tm):
    m, d = xw.shape
    ng = d // LANES
    row = pl.BlockSpec((tm, d), lambda i: (i, 0))
    lane_rows = pl.BlockSpec((tm * ng, LANES), lambda i: (i, 0))
    full = lambda a: pl.BlockSpec(a.shape, lambda i: (0, 0))
    w0 = w0.reshape(1, d)
    a0 = a0.reshape(1, d)
    consts = [w1, w2, a1, a2, g1, g2, w0, a0]
    return pl.pallas_call(
        _rw_lora_kernel,
        grid=(m // tm,),
        in_specs=[row, row, row] + [full(a) for a in consts],
        out_specs=[lane_rows, lane_rows, row],
        out_shape=[jax.ShapeDtypeStruct((m * ng, LANES), F32)] * 2 + [jax.ShapeDtypeStruct((m, d), F32)],
        compiler_params=_cp("parallel"),
        name="rw_lora",
    )(xw, xa, xg, *consts)


def _mm_lane_rows_kernel(x_ref, w_ref, o_ref):
    _store_lane_rows(o_ref, _dot(x_ref[...], w_ref[...]))


def _mm_lane_rows(x, w, lead, *, tm):
    m, k = x.shape
    n = w.shape[2]
    ng = n // LANES
    return pl.pallas_call(
        _mm_lane_rows_kernel,
        grid=(m // tm,),
        in_specs=[pl.BlockSpec((tm, k), lambda i: (i, 0)),
                  pl.BlockSpec((None, k, n), lambda i: (lead, 0, 0))],
        out_specs=pl.BlockSpec((tm * ng, LANES), lambda i: (i, 0)),
        out_shape=jax.ShapeDtypeStruct((m * ng, LANES), F32),
        compiler_params=_cp("parallel"),
        name="mm_lane_rows",
    )(x, w)


def _lane_group_sum(x):
    axis = x.ndim - 1
    x = x + pltpu.roll(x, RW_HEADS, axis=axis)
    return x + pltpu.roll(x, 2 * RW_HEADS, axis=axis)


def _rw_scan_kernel(r_ref, w_ref, k_ref, a_ref, v_ref, kkw_ref, kaw_ref, rkw_ref, lnw_ref, lnb_ref, s0_ref,
                    y_ref, sout_ref, s00_ref, s01_ref, s10_ref, s11_ref, kk_ref, bb_ref, km_ref, bonus_ref, vt_ref,
                    *, tt, nblk):
    tb = pl.program_id(1)
    ng = D_MODEL // LANES
    nh = RW_HEAD // 2
    halves = (slice(0, nh), slice(nh, RW_HEAD))
    state = ((s00_ref, s01_ref), (s10_ref, s11_ref))

    @pl.when(tb == 0)
    def _():
        for b in range(2):
            for h in range(2):
                state[b][h][...] = s0_ref[b, :, halves[h], :]

    def head_total(x):
        return _lane_group_sum(jnp.sum(x, axis=2, keepdims=True))

    for c0 in range(0, tt, SUBLANES):
        ts = slice(c0, min(c0 + SUBLANES, tt))
        k = k_ref[:, ts]
        a = a_ref[:, ts]
        kk = k * kkw_ref[...]
        kk = kk * lax.rsqrt(jnp.maximum(head_total(kk * kk), 1e-24))
        km = k * (1.0 + (a - 1.0) * kaw_ref[...])
        kk_ref[:, ts] = kk
        bb_ref[:, ts] = kk * a
        km_ref[:, ts] = km
        bonus_ref[:, ts] = head_total(r_ref[:, ts] * km * rkw_ref[...])

    rc = lax.broadcasted_iota(jnp.int32, (LANES, 4 * LANES), 0)
    cc = lax.broadcasted_iota(jnp.int32, (LANES, 4 * LANES), 1)
    spread = jnp.where((cc // LANES == rc // RW_HEADS) & (cc % RW_HEADS == rc % RW_HEADS), 1.0, 0.0).astype(BF16)
    for b in range(2):
        x = v_ref[b].reshape(tt * ng, LANES)
        hi = x.astype(BF16)
        rest = x - hi.astype(F32)
        mid = rest.astype(BF16)
        lo = (rest - mid.astype(F32)).astype(BF16)
        rep = _dot(hi, spread) + _dot(mid, spread) + _dot(lo, spread)
        for jv in range(4):
            vt_ref[b, pl.ds(jv, tt * ng, stride=4), :] = rep[:, jv * LANES:(jv + 1) * LANES]

    def value_rows(b, t, h):
        return vt_ref[b, pl.ds(pl.multiple_of(t * RW_HEAD + h * nh, nh), nh), :]

    def emit(b, t, y_parts):
        ys = [_lane_group_sum(p) for p in y_parts]
        mu = (jnp.sum(ys[0], axis=0, keepdims=True) + jnp.sum(ys[1], axis=0, keepdims=True)) * (1.0 / RW_HEAD)
        ds = [y - mu for y in ys]
        var = (jnp.sum(ds[0] * ds[0], axis=0, keepdims=True)
               + jnp.sum(ds[1] * ds[1], axis=0, keepdims=True)) * (1.0 / RW_HEAD)
        inv = lax.rsqrt(var + RW_LN_EPS)
        for h, (rows, d) in enumerate(zip(halves, ds)):
            y_ref[b, t, rows, :] = (d * inv * lnw_ref[rows, :] + lnb_ref[rows, :]
                                    + bonus_ref[b, t] * value_rows(b, t, h))

    def update(b, h, t, t_next, sa):
        st = state[b][h]
        vv = value_rows(b, t, h)
        y_acc = jnp.zeros((nh, LANES), F32)
        sa_acc = jnp.zeros((nh, LANES), F32)
        for g in range(ng):
            sg = st[g] * w_ref[b, t, g:g + 1, :] + sa * bb_ref[b, t, g:g + 1, :] + vv * km_ref[b, t, g:g + 1, :]
            st[g] = sg
            y_acc = y_acc + sg * r_ref[b, t, g:g + 1, :]
            sa_acc = sa_acc - sg * kk_ref[b, t_next, g:g + 1, :]
        return sa_acc, y_acc

    def first_dot(b, h):
        acc = jnp.zeros((nh, LANES), F32)
        for g in range(ng):
            acc = acc - state[b][h][g] * kk_ref[b, 0, g:g + 1, :]
        return acc

    def step(t, carry):
        sa00, sa01, sa10_parts, sa11_parts, y10_parts, y11_parts = carry
        t_next = jnp.minimum(t + 1, tt - 1)
        sa10 = _lane_group_sum(sa10_parts)
        sa11 = _lane_group_sum(sa11_parts)
        emit(1, jnp.maximum(t - 1, 0), (y10_parts, y11_parts))
        sa00_parts, y00_parts = update(0, 0, t, t_next, sa00)
        sa01_parts, y01_parts = update(0, 1, t, t_next, sa01)
        sa00 = _lane_group_sum(sa00_parts)
        sa10_parts, y10_parts = update(1, 0, t, t_next, sa10)
        sa01 = _lane_group_sum(sa01_parts)
        emit(0, t, (y00_parts, y01_parts))
        sa11_parts, y11_parts = update(1, 1, t, t_next, sa11)
        return sa00, sa01, sa10_parts, sa11_parts, y10_parts, y11_parts

    zero = jnp.zeros((nh, LANES), F32)
    init = (_lane_group_sum(first_dot(0, 0)), _lane_group_sum(first_dot(0, 1)),
            first_dot(1, 0), first_dot(1, 1), zero, zero)
    last = lax.fori_loop(0, tt, step, init)
    emit(1, tt - 1, last[4:])

    @pl.when(tb == nblk - 1)
    def _():
        for b in range(2):
            for h in range(2):
                sout_ref[b, :, halves[h], :] = state[b][h][...]


def _rw_scan(r, w, k, a, v, kkw, kaw, rkw, lnw, lnb, s0, *, tt):
    b, t, ng, _ = r.shape
    nblk = t // tt
    kblk = pl.BlockSpec((2, tt, ng, LANES), lambda gi, ti: (gi, ti, 0, 0))
    vblk = pl.BlockSpec((2, tt, RW_HEAD, LANES), lambda gi, ti: (gi, ti, 0, 0))
    kconst = pl.BlockSpec((ng, LANES), lambda gi, ti: (0, 0))
    vconst = pl.BlockSpec((RW_HEAD, LANES), lambda gi, ti: (0, 0))
    sblk = pl.BlockSpec((2, ng, RW_HEAD, LANES), lambda gi, ti: (gi, 0, 0, 0))
    return pl.pallas_call(
        functools.partial(_rw_scan_kernel, tt=tt, nblk=nblk),
        grid=(b // 2, nblk),
        in_specs=[kblk, kblk, kblk, kblk, kblk, kconst, kconst, kconst, vconst, vconst, sblk],
        out_specs=[vblk, sblk],
        out_shape=[jax.ShapeDtypeStruct((b, t, RW_HEAD, LANES), F32),
                   jax.ShapeDtypeStruct((b, ng, RW_HEAD, LANES), F32)],
        scratch_shapes=([pltpu.VMEM((ng, RW_HEAD // 2, LANES), F32)] * 4 + [pltpu.VMEM((2, tt, ng, LANES), F32)] * 3
                        + [pltpu.VMEM((2, tt, 1, LANES), F32), pltpu.VMEM((2, tt * RW_HEAD, LANES), F32)]),
        compiler_params=_cp("parallel", "arbitrary"),
        name="rw_scan",
    )(r, w, k, a, v, kkw, kaw, rkw, lnw, lnb, s0)


def _head_minor(vec):
    return vec.reshape(RW_HEADS, RW_HEAD).T.reshape(1, D_MODEL)


def _head_minor_matrix():
    new = jnp.arange(D_MODEL)
    old = (new % RW_HEADS) * RW_HEAD + new // RW_HEADS
    return (jnp.arange(D_MODEL)[:, None] == old[None, :]).astype(BF16)[None]


def _value_tiles(vec):
    lead = vec.shape[:-1]
    x = vec.reshape(*lead, RW_HEAD, 1, RW_HEADS)
    return jnp.broadcast_to(x, (*lead, RW_HEAD, LANES // RW_HEADS, RW_HEADS)).reshape(*lead, RW_HEAD, LANES)


def _state_to_scan(s):
    bsz = s.shape[0]
    s = s.reshape(bsz, RW_HEADS, RW_HEAD, RW_HEAD // 4, 4).transpose(0, 3, 2, 4, 1)
    return s.reshape(bsz, RW_HEAD // 4, RW_HEAD, LANES)


def _state_from_scan(s):
    bsz = s.shape[0]
    s = s.reshape(bsz, RW_HEAD // 4, RW_HEAD, 4, RW_HEADS).transpose(0, 4, 2, 1, 3)
    return s.reshape(bsz, RW_HEADS, RW_HEAD, RW_HEAD)


def _prep_weights(p):
    w = {}
    w["mx_w_in"] = p["mx_w_in"]
    gpad = GLA_GATE_PAD - GLA_GATE_RANK
    w["mx_w_low"] = jnp.pad(p["mx_w_in"][:, :, EVEN_MAIN_WIDTH:], ((0, 0), (0, 0), (0, gpad))).astype(BF16)
    w["gla_w_a2"] = jnp.pad(p["gla_w_a2"], ((0, 0), (0, gpad), (0, 0))).astype(BF16)
    w["mx_w_out"] = p["mx_w_out"].astype(BF16)
    n_odd = p["rw_w_rkv"].shape[0]
    perm = _head_minor_matrix()
    permute = lambda rows: _mm(rows, perm, 0, D_MODEL, tm=min(rows.shape[0], 1024), tn=1024, out_dtype=BF16)
    w["rw_w_rkv"] = permute(p["rw_w_rkv"].reshape(-1, D_MODEL)).reshape(-1, D_MODEL, D_MODEL)
    pad_c = lambda a: jnp.pad(a, ((0, 0), (0, 0), (0, RW_RANK_PAD - a.shape[2]))).astype(BF16)
    pad_r = lambda a: jnp.pad(a, ((0, 0), (0, RW_RANK_PAD - a.shape[1]), (0, 0)))
    second = jnp.concatenate([pad_r(p["rw_w2"]), pad_r(p["rw_a2"]), p["rw_g2"]], axis=1)
    second = permute(second.reshape(-1, D_MODEL)).reshape(n_odd, -1, D_MODEL)
    w["rw_w1"], w["rw_a1"], w["rw_g1"] = pad_c(p["rw_w1"]), pad_c(p["rw_a1"]), p["rw_g1"].astype(BF16)
    w["rw_w2"] = second[:, :RW_RANK_PAD]
    w["rw_a2"] = second[:, RW_RANK_PAD:2 * RW_RANK_PAD]
    w["rw_g2"] = second[:, 2 * RW_RANK_PAD:]
    w["rw_w_o"] = (p["rw_w_o"].reshape(n_odd, RW_HEADS, RW_HEAD, D_MODEL).transpose(0, 2, 1, 3)
                   .reshape(n_odd, D_MODEL, D_MODEL).astype(BF16))
    for name in ("rw_w0", "rw_a0", "rw_k_k", "rw_k_a"):
        w[name] = [_head_minor(p[name][j]) for j in range(n_odd)]
    w["rw_r_k"] = [_head_minor(p["rw_r_k"][j].reshape(-1)) for j in range(n_odd)]
    for name in ("rw_ln_w", "rw_ln_b"):
        w[name] = [_value_tiles(_head_minor(p[name][j])[0]) for j in range(n_odd)]
    w["ca_w_q"] = p["ca_w_q"].astype(BF16)
    w["ca_w_kv"] = p["ca_w_kv"]
    w["ca_w_o"] = p["ca_w_o"].astype(BF16)
    w["ffn_w_up"] = p["ffn_w_up"]
    w["ffn_conv_b"] = p["ffn_conv_b"][:, None, :]
    w["ffn_down"] = p["ffn_w_down"].astype(BF16)
    for name in ("sb_bias", "gla_b_a", "gla_norm_w", "rw_mix", "ffn_conv_w", "ln_w", "ln_b"):
        w[name] = p[name]
    return w


def _trunk(x, mem_k, mem_v, mem_cols, sb_past, page_table, gla_s0, rw_s0, rw_shift0, conv0, w, *, prompt):
    bsz, seq, _ = x.shape
    m = bsz * seq
    tm = 512 if prompt else m
    x = x.reshape(m, D_MODEL)
    new_k, new_v, new_gla, new_rw, new_shift, new_conv = [], [], [], [], [], []
    for i in range(DEPTH):
        j = i // 2
        lw, lb = w["ln_w"][i], w["ln_b"][i]
        if i % 2 == 0:
            proj = _mm(x, w["mx_w_in"], j, EVEN_MAIN_WIDTH, tm=min(m, 1024), tn=768)
            low = _mm(x, w["mx_w_low"], j, GLA_GATE_PAD, tm=min(m, 1024), tn=GLA_GATE_PAD)
            ka = proj[:, SB_WIDTH:2 * SB_WIDTH].reshape(bsz, seq, SB_HEADS, SB_HEAD_DIM)
            va = proj[:, 2 * SB_WIDTH:3 * SB_WIDTH].reshape(bsz, seq, SB_HEADS, SB_HEAD_DIM)
            new_k.append(ka)
            new_v.append(va)
            if prompt:
                oa = _sb_prompt(proj, w["sb_bias"][j], batch=bsz, seq=seq, tq=512, tk=256)
            else:
                cache_k, cache_v = sb_past
                n_phys = cache_k.shape[1]
                rows = PAGE_SIZE * SB_HEADS
                qa = proj[:, :SB_WIDTH].reshape(bsz, seq, SB_HEADS, SB_HEAD_DIM).transpose(0, 2, 1, 3)
                qa = (qa.reshape(bsz, SB_HEADS * seq, SB_HEAD_DIM) * (SB_HEAD_DIM ** -0.5)).astype(BF16)
                qa = jnp.pad(qa, ((0, 0), (0, LANES - SB_HEADS * seq), (0, 0)))
                bias_cols = jnp.pad(jnp.repeat(w["sb_bias"][j], seq), (0, LANES - SB_HEADS * seq)).reshape(1, LANES)
                pad_new = lambda a: jnp.pad(a.reshape(bsz, seq * SB_HEADS, SB_HEAD_DIM),
                                            ((0, 0), (0, rows - seq * SB_HEADS), (0, 0)))
                oa = _sb_sample(qa, bias_cols, pad_new(ka), pad_new(va),
                                cache_k[j].reshape(n_phys, rows, SB_HEAD_DIM),
                                cache_v[j].reshape(n_phys, rows, SB_HEAD_DIM), page_table, pages=8)
                oa = oa[:, :SB_HEADS * seq].reshape(bsz, SB_HEADS, seq, SB_HEAD_DIM).transpose(0, 2, 1, 3)
                oa = oa.reshape(m, SB_WIDTH).astype(BF16)
            ob, s_t = _gla(proj, low, w["gla_w_a2"][j], w["gla_b_a"][j], w["gla_norm_w"][j],
                           gla_s0[j].transpose(0, 1, 3, 2), batch=bsz, seq=seq, rows=min(seq, 256))
            new_gla.append(s_t.transpose(0, 1, 3, 2))
            o = jnp.concatenate([oa, ob.astype(BF16)], axis=-1)
            x = _mm_ln(o, w["mx_w_out"][j], x, lw[0], lb[0], tm=tm, tk=o.shape[1])
        else:
            x3 = x.reshape(bsz, seq, D_MODEL)
            new_shift.append(x3[:, -1])
            x_prev = jnp.concatenate([rw_shift0[j][:, None, :], x3[:, :-1]], axis=1).reshape(m, D_MODEL)
            xr, xw, xk, xv, xa, xg = _rw_mix(x, x_prev, w["rw_mix"][j], tm=tm)
            r = _mm_lane_rows(xr, w["rw_w_rkv"], 3 * j, tm=min(m, 512))
            k = _mm_lane_rows(xk, w["rw_w_rkv"], 3 * j + 1, tm=min(m, 512))
            v = _mm_lane_rows(xv, w["rw_w_rkv"], 3 * j + 2, tm=min(m, 512))
            dec, a, gate = _rw_lora(xw, xa, xg, w["rw_w1"][j], w["rw_w2"][j], w["rw_a1"][j], w["rw_a2"][j],
                                    w["rw_g1"][j], w["rw_g2"][j], w["rw_w0"][j], w["rw_a0"][j], tm=min(m, 256))
            rows = lambda z: z.reshape(bsz, seq, D_MODEL // LANES, LANES)
            krow = lambda z: z.reshape(D_MODEL // LANES, LANES)
            y, s_new = _rw_scan(rows(r), rows(dec), rows(k), rows(a), rows(v),
                                krow(w["rw_k_k"][j]), krow(w["rw_k_a"][j]), krow(w["rw_r_k"][j]),
                                w["rw_ln_w"][j], w["rw_ln_b"][j],
                                _state_to_scan(rw_s0[j]), tt=min(seq, 32))
            new_rw.append(_state_from_scan(s_new))
            y = y[..., :RW_HEADS].reshape(m, D_MODEL)
            x = _mm_ln(y, w["rw_w_o"][j], x, lw[0], lb[0], tm=tm, tk=D_MODEL, gate=gate)
        x, xb = _cross_attn(x, mem_k[i], mem_v[i], mem_cols[0], mem_cols[1], w["ca_w_q"][i], w["ca_w_o"][i],
                            lw[1], lb[1], batch=bsz, tm=min(seq, 512))
        f2 = 2 * FFN_HIDDEN
        cw, cb = w["ffn_conv_w"][i], w["ffn_conv_b"][i]
        if prompt:
            c0 = jnp.pad(conv0[i], ((0, 0), (SUBLANES - conv0[i].shape[1], 0), (0, 0)))
            c, cs = _ffn_up(xb, w["ffn_w_up"], i, cw, cb, c0, tm=min(seq, 2048), tn=256, shift=1, seqs=bsz)
            new_conv.append(cs[:, -2:, :])
        else:
            c0 = conv0[i].transpose(1, 0, 2).reshape(1, 2 * bsz, f2)
            xt = xb.reshape(bsz, seq, D_MODEL).transpose(1, 0, 2).reshape(m, D_MODEL)
            c, cs = _ffn_up(xt, w["ffn_w_up"], i, cw, cb, c0, tm=m, tn=256, shift=bsz, seqs=1)
            c = c.reshape(seq, bsz, f2).transpose(1, 0, 2).reshape(m, f2)
            new_conv.append(cs.reshape(2, bsz, f2).transpose(1, 0, 2))
        x = _glu_ln(c, w["ffn_down"], i, x, lw[2], lb[2], tm=min(m, 256), kchunk=1408)
    return (x.reshape(bsz, seq, D_MODEL), jnp.stack(new_k), jnp.stack(new_v), jnp.stack(new_gla),
            jnp.stack(new_rw), jnp.stack(new_shift), jnp.stack(new_conv))


def kernel(x_prompt, x_sample, mem_prompt, cache_sb_k, cache_sb_v, page_table, state_gla, state_rwkv,
           state_rwkv_shift, state_ffn_conv, cache_mem_k, cache_mem_v, mx_w_in, sb_bias, gla_w_a2, gla_b_a,
           gla_norm_w, mx_w_out, rw_mix, rw_w_rkv, rw_w0, rw_w1, rw_w2, rw_a0, rw_a1, rw_a2, rw_g1, rw_g2,
           rw_k_k, rw_k_a, rw_r_k, rw_ln_w, rw_ln_b, rw_w_o, ca_w_q, ca_w_kv, ca_w_o, ffn_w_up, ffn_conv_w,
           ffn_conv_b, ffn_w_down, ln_w, ln_b):
    w = _prep_weights(dict(
        mx_w_in=mx_w_in, sb_bias=sb_bias, gla_w_a2=gla_w_a2, gla_b_a=gla_b_a, gla_norm_w=gla_norm_w,
        mx_w_out=mx_w_out, rw_mix=rw_mix, rw_w_rkv=rw_w_rkv, rw_w0=rw_w0, rw_w1=rw_w1, rw_w2=rw_w2,
        rw_a0=rw_a0, rw_a1=rw_a1, rw_a2=rw_a2, rw_g1=rw_g1, rw_g2=rw_g2, rw_k_k=rw_k_k, rw_k_a=rw_k_a,
        rw_r_k=rw_r_k, rw_ln_w=rw_ln_w, rw_ln_b=rw_ln_b, rw_w_o=rw_w_o, ca_w_q=ca_w_q, ca_w_kv=ca_w_kv,
        ca_w_o=ca_w_o, ffn_w_up=ffn_w_up, ffn_conv_w=ffn_conv_w, ffn_conv_b=ffn_conv_b,
        ffn_w_down=ffn_w_down, ln_w=ln_w, ln_b=ln_b))
    n_even = (DEPTH + 1) // 2
    n_odd = DEPTH // 2
    b, _, _ = x_prompt.shape
    db = x_sample.shape[0]

    mem_rows = mem_prompt.reshape(b * N_MEM, D_MODEL)
    mem_kv = [_mm(mem_rows, w["ca_w_kv"], i, 2 * MEM_WIDTH, tm=b * N_MEM, tn=512).reshape(b, N_MEM, 2 * MEM_WIDTH)
              for i in range(DEPTH)]
    mem_shape = (DEPTH, b, N_MEM, MEM_HEADS, MEM_HEAD_DIM)
    mem_k_p = jnp.stack([kv[..., :MEM_WIDTH] for kv in mem_kv]).reshape(mem_shape)
    mem_v_p = jnp.stack([kv[..., MEM_WIDTH:] for kv in mem_kv]).reshape(mem_shape)
    gla0 = jnp.zeros((n_even, b, GLA_HEADS, GLA_DK, GLA_DV), F32)
    rw0 = jnp.zeros((n_odd, b, RW_HEADS, RW_HEAD, RW_HEAD), F32)
    sh0 = jnp.zeros((n_odd, b, D_MODEL), F32)
    cv0 = jnp.zeros((DEPTH, b, 2, 2 * FFN_HIDDEN), F32)
    y_p, sbk_p, sbv_p, gla_p, rw_p, sh_p, conv_p = _trunk(
        x_prompt, mem_kv, mem_kv, (0, 1), None, None, gla0, rw0, sh0, cv0, w, prompt=True)

    mk = cache_mem_k.reshape(DEPTH, db, N_MEM, MEM_WIDTH)
    mv = cache_mem_v.reshape(DEPTH, db, N_MEM, MEM_WIDTH)
    y_s, sbk_s, sbv_s, gla_s, rw_s, sh_s, conv_s = _trunk(
        x_sample, mk, mv, (0, 0), (cache_sb_k, cache_sb_v), page_table, state_gla, state_rwkv,
        state_rwkv_shift, state_ffn_conv, w, prompt=False)
    return (y_p, y_s, sbk_p, sbv_p, gla_p, rw_p, sh_p, conv_p, mem_k_p, mem_v_p,
            sbk_s, sbv_s, gla_s, rw_s, sh_s, conv_s)
```

```python
import functools

import jax
import jax.numpy as jnp
from jax import lax
from jax.experimental import pallas as pl
from jax.experimental.pallas import tpu as pltpu

F32 = jnp.float32
BF16 = jnp.bfloat16

D_MODEL = 2048
DEPTH = 2
PAGE_SIZE = 128
SB_HEADS = 8
SB_HEAD_DIM = 128
SB_WIDTH = SB_HEADS * SB_HEAD_DIM
GLA_HEADS = 4
GLA_DK = 128
GLA_DV = 256
GLA_K_WIDTH = GLA_HEADS * GLA_DK
GLA_V_WIDTH = GLA_HEADS * GLA_DV
GLA_GATE_RANK = 16
GLA_GATE_PAD = 128
GLA_GATE_NORMALIZER = 16.0
GLA_CHUNK = 64
GLA_NORM_EPS = 1e-5
EVEN_MAIN_WIDTH = 3 * SB_WIDTH + 2 * GLA_K_WIDTH + 2 * GLA_V_WIDTH
RW_HEAD = 64
RW_HEADS = D_MODEL // RW_HEAD
RW_LN_EPS = 64e-5
RW_RANK_PAD = 128
N_MEM = 256
MEM_HEADS = 4
MEM_HEAD_DIM = 128
MEM_WIDTH = MEM_HEADS * MEM_HEAD_DIM
FFN_HIDDEN = 5504
FFN_ROW_CHUNK = 512
LN_EPS = 1e-5
ALPHA = (2.0 * DEPTH) ** 0.25

LANES = 128
SUBLANES = 8
VMEM_LIMIT_MB = 56

NT_DIMS = (((1,), (1,)), ((), ()))
TN_DIMS = (((0,), (0,)), ((), ()))


def _row_dtype(rows):
    return BF16 if rows % (2 * SUBLANES) == 0 else F32


def _cp(*sem):
    return pltpu.CompilerParams(dimension_semantics=sem, vmem_limit_bytes=VMEM_LIMIT_MB * 1024 * 1024)


def _dot(a, b):
    return jnp.dot(a, b, preferred_element_type=F32)


def _dot_nt(a, b):
    return lax.dot_general(a, b, NT_DIMS, preferred_element_type=F32)


def _softplus_neg_abs(z):
    return jnp.log1p(jnp.exp(-jnp.abs(z)))


def _log_sigmoid(z):
    return jnp.minimum(z, 0.0) - _softplus_neg_abs(z)


def _sigmoid(z):
    return 1.0 / (1.0 + jnp.exp(-z))


def _split_bf16(x):
    hi = x.astype(BF16)
    lo = (x - hi.astype(F32)).astype(BF16)
    return hi, lo


def _layer_norm(y, w, b):
    mu = jnp.mean(y, axis=-1, keepdims=True)
    d = y - mu
    var = jnp.mean(d * d, axis=-1, keepdims=True)
    return d * lax.rsqrt(var + LN_EPS) * w + b


def _mm_kernel(x_ref, w_ref, o_ref, wb_ref):
    @pl.when(pl.program_id(1) == 0)
    def _():
        wb_ref[...] = w_ref[...].astype(BF16)

    o_ref[...] = _dot(x_ref[...].astype(BF16), wb_ref[...]).astype(o_ref.dtype)


def _mm(x, w, lead, n_out, *, tm, tn, out_dtype=F32):
    m, k = x.shape
    return pl.pallas_call(
        _mm_kernel,
        grid=(n_out // tn, m // tm),
        in_specs=[pl.BlockSpec((tm, k), lambda j, i: (i, 0)),
                  pl.BlockSpec((None, k, tn), lambda j, i: (lead, 0, j))],
        out_specs=pl.BlockSpec((tm, tn), lambda j, i: (i, j)),
        out_shape=jax.ShapeDtypeStruct((m, n_out), out_dtype),
        scratch_shapes=[pltpu.VMEM((k, tn), BF16)],
        compiler_params=_cp("parallel", "arbitrary"),
        name="mm",
    )(x, w)


def _mm_ln_kernel(*refs, nk, gated):
    if gated:
        x_ref, g_ref, w_ref, res_ref, lw_ref, lb_ref, o_ref = refs[:7]
        x = (x_ref[...] * g_ref[...]).astype(BF16)
    else:
        x_ref, w_ref, res_ref, lw_ref, lb_ref, o_ref = refs[:6]
        x = x_ref[...].astype(BF16)
    part = _dot(x, w_ref[...])

    def finish(h):
        o_ref[...] = _layer_norm(ALPHA * res_ref[...] + h, lw_ref[...], lb_ref[...])

    if nk == 1:
        finish(part)
        return
    acc_ref = refs[-1]
    kk = pl.program_id(1)

    @pl.when(kk == 0)
    def _():
        acc_ref[...] = part

    @pl.when(kk > 0)
    def _():
        acc_ref[...] += part

    @pl.when(kk == nk - 1)
    def _():
        finish(acc_ref[...])


def _mm_ln(x, w, res, lw, lb, *, tm, tk, gate=None):
    m, k = x.shape
    n = w.shape[1]
    nk = k // tk
    gated = gate is not None
    xs = pl.BlockSpec((tm, tk), lambda i, kk: (i, kk))
    row = pl.BlockSpec((tm, n), lambda i, kk: (i, 0))
    vec = pl.BlockSpec((1, n), lambda i, kk: (0, 0))
    in_specs = [xs] + ([xs] if gated else []) + [pl.BlockSpec((tk, n), lambda i, kk: (kk, 0)), row, vec, vec]
    args = [x] + ([gate] if gated else []) + [w, res, lw.reshape(1, n), lb.reshape(1, n)]
    return pl.pallas_call(
        functools.partial(_mm_ln_kernel, nk=nk, gated=gated),
        grid=(m // tm, nk),
        in_specs=in_specs,
        out_specs=row,
        out_shape=jax.ShapeDtypeStruct((m, n), F32),
        scratch_shapes=[pltpu.VMEM((tm, n), F32)] if nk > 1 else [],
        compiler_params=_cp("parallel", "arbitrary"),
        name="mm_ln",
    )(*args)


def _glu_ln_kernel(cv_ref, cg_ref, w_ref, res_ref, lw_ref, lb_ref, o_ref, *, kchunk):
    k = cv_ref.shape[1]
    h = None
    for c0 in range(0, k, kchunk):
        c1 = min(c0 + kchunk, k)
        g = cg_ref[:, c0:c1].astype(F32)
        act = (g * _sigmoid(g) * cv_ref[:, c0:c1].astype(F32)).astype(BF16)
        part = _dot(act, w_ref[c0:c1, :])
        h = part if h is None else h + part
    o_ref[...] = _layer_norm(ALPHA * res_ref[...] + h, lw_ref[...], lb_ref[...])


def _glu_ln(c, w, lead, res, lw, lb, *, tm, kchunk):
    m = c.shape[0]
    _, k, n = w.shape
    row = pl.BlockSpec((tm, n), lambda i: (i, 0))
    vec = pl.BlockSpec((1, n), lambda i: (0, 0))
    return pl.pallas_call(
        functools.partial(_glu_ln_kernel, kchunk=kchunk),
        grid=(m // tm,),
        in_specs=[pl.BlockSpec((tm, k), lambda i: (i, 0)), pl.BlockSpec((tm, k), lambda i: (i, 1)),
                  pl.BlockSpec((None, k, n), lambda i: (lead, 0, 0), pipeline_mode=pl.Buffered(1)),
                  row, vec, vec],
        out_specs=row,
        out_shape=jax.ShapeDtypeStruct((m, n), F32),
        compiler_params=_cp("parallel"),
        name="glu_ln",
    )(c, c, w, res, lw.reshape(1, n), lb.reshape(1, n))


def _ca_kernel(x_ref, wq_ref, mk_ref, mv_ref, wo_ref, lw_ref, lb_ref, o_ref, ob_ref):
    x = x_ref[...]
    q = _dot(x.astype(BF16), wq_ref[...])
    heads = []
    for h in range(MEM_HEADS):
        cols = slice(h * MEM_HEAD_DIM, (h + 1) * MEM_HEAD_DIM)
        s = _dot_nt(q[:, cols].astype(BF16), mk_ref[:, cols].astype(BF16)) * (MEM_HEAD_DIM ** -0.5)
        e = jnp.exp(s - jnp.max(s, axis=-1, keepdims=True))
        p = e / jnp.sum(e, axis=-1, keepdims=True)
        heads.append(_dot(p.astype(BF16), mv_ref[:, cols].astype(BF16)))
    o = jnp.concatenate(heads, axis=-1)
    y = _layer_norm(ALPHA * x + _dot(o.astype(BF16), wo_ref[...]), lw_ref[...], lb_ref[...])
    o_ref[...] = y
    ob_ref[...] = y.astype(ob_ref.dtype)


def _cross_attn(x, mem_k, mem_v, k_col, v_col, wq, wo, lw, lb, *, batch, tm):
    m = x.shape[0]
    tiles = m // batch // tm
    vec = pl.BlockSpec((1, D_MODEL), lambda b, i: (0, 0))
    row = pl.BlockSpec((tm, D_MODEL), lambda b, i: (b * tiles + i, 0))
    return pl.pallas_call(
        _ca_kernel,
        grid=(batch, tiles),
        in_specs=[row,
                  pl.BlockSpec((D_MODEL, MEM_WIDTH), lambda b, i: (0, 0)),
                  pl.BlockSpec((None, N_MEM, MEM_WIDTH), lambda b, i: (b, 0, k_col)),
                  pl.BlockSpec((None, N_MEM, MEM_WIDTH), lambda b, i: (b, 0, v_col)),
                  pl.BlockSpec((MEM_WIDTH, D_MODEL), lambda b, i: (0, 0)),
                  vec, vec],
        out_specs=[row, row],
        out_shape=[jax.ShapeDtypeStruct((m, D_MODEL), F32), jax.ShapeDtypeStruct((m, D_MODEL), _row_dtype(tm))],
        compiler_params=_cp("parallel", "arbitrary"),
        name="cross_attn",
    )(x, wq, mem_k, mem_v, wo, lw.reshape(1, -1), lb.reshape(1, -1))


def _ffn_up_kernel(x_ref, w_ref, cw_ref, cb_ref, c0_ref, c_ref, cs_ref, carry_ref, buf,
                   *, tm, shift, carry_rows, tiles_per_seq):
    i = pl.program_id(0)
    j = pl.program_id(1)
    cr = carry_rows

    @pl.when(i % tiles_per_seq == 0)
    def _():
        buf[0:cr, :] = c0_ref[...]

    @pl.when(i % tiles_per_seq != 0)
    def _():
        buf[0:cr, :] = carry_ref[j]

    wb = w_ref[...].astype(BF16)
    rc = min(tm, FFN_ROW_CHUNK)
    for r0 in range(0, tm, rc):
        buf[cr + r0:cr + r0 + rc, :] = _dot(x_ref[r0:r0 + rc, :].astype(BF16), wb)
        c = (cw_ref[0:1, :] * buf[cr - 2 * shift + r0:cr - 2 * shift + r0 + rc, :]
             + cw_ref[1:2, :] * buf[cr - shift + r0:cr - shift + r0 + rc, :]
             + cw_ref[2:3, :] * buf[cr + r0:cr + r0 + rc, :] + cb_ref[...])
        c_ref[r0:r0 + rc, :] = c.astype(c_ref.dtype)
    last = buf[tm:tm + cr, :]
    cs_ref[...] = last
    carry_ref[j] = last


def _ffn_up(xb, w, lead, cw, cb, c0, *, tm, tn, shift, seqs):
    m, k = xb.shape
    n = w.shape[2]
    carry_rows = c0.shape[1]
    tiles_per_seq = m // seqs // tm
    col = lambda i, j: (0, j)
    c, tails = pl.pallas_call(
        functools.partial(_ffn_up_kernel, tm=tm, shift=shift, carry_rows=carry_rows, tiles_per_seq=tiles_per_seq),
        grid=(m // tm, n // tn),
        in_specs=[pl.BlockSpec((tm, k), lambda i, j: (i, 0)),
                  pl.BlockSpec((None, k, tn), lambda i, j: (lead, 0, j)),
                  pl.BlockSpec((3, tn), col), pl.BlockSpec((1, tn), col),
                  pl.BlockSpec((None, carry_rows, tn), lambda i, j: (i // tiles_per_seq, 0, j))],
        out_specs=[pl.BlockSpec((tm, tn), lambda i, j: (i, j)),
                   pl.BlockSpec((None, carry_rows, tn), lambda i, j: (i, 0, j))],
        out_shape=[jax.ShapeDtypeStruct((m, n), BF16), jax.ShapeDtypeStruct((m // tm, carry_rows, n), F32)],
        scratch_shapes=[pltpu.VMEM((n // tn, carry_rows, tn), F32), pltpu.VMEM((tm + carry_rows, tn), F32)],
        compiler_params=_cp("arbitrary", "arbitrary"),
        name="ffn_up",
    )(xb, w, cw, cb, c0)
    return c, tails[tiles_per_seq - 1::tiles_per_seq]


def _sb_weights(z, carry, tri, mask):
    sp = jnp.log(1.0 + jnp.exp(-jnp.abs(z)))
    log_beta = jnp.minimum(z, 0.0) - sp
    log_keep = jnp.minimum(-z, 0.0) - sp
    if mask is not None:
        log_keep = jnp.where(mask, log_keep, 0.0)
    hi, lo = _split_bf16(log_keep)
    a = jnp.exp(log_beta + _dot(hi, tri) + _dot(lo, tri) + carry)
    if mask is not None:
        a = jnp.where(mask, a, 0.0)
    return a, carry + jnp.sum(log_keep, axis=1, keepdims=True)


def _sb_prompt_kernel(bias_ref, q_ref, k_ref, v_ref, o_ref, *, tq, tk):
    h = pl.program_id(1)
    qi = pl.program_id(2)
    bias = bias_ref[h]
    nsub = tq // tk
    q = (q_ref[...] * (SB_HEAD_DIM ** -0.5)).astype(BF16)
    r = lax.broadcasted_iota(jnp.int32, (tk, tk), 0)
    c = lax.broadcasted_iota(jnp.int32, (tk, tk), 1)
    tri = jnp.where(r > c, 1.0, 0.0).astype(BF16)
    trow = lax.broadcasted_iota(jnp.int32, (tq, tk), 0)
    scol = lax.broadcasted_iota(jnp.int32, (tq, tk), 1)

    def block(start, acc, carry, mask):
        k = k_ref[pl.ds(start, tk), :].astype(BF16)
        v = v_ref[pl.ds(start, tk), :].astype(BF16)
        a, carry = _sb_weights(_dot_nt(q, k) + bias, carry, tri, mask)
        return acc + _dot(a.astype(BF16), v), carry

    acc = jnp.zeros((tq, SB_HEAD_DIM), F32)
    carry = jnp.zeros((tq, 1), F32)
    for d in reversed(range(nsub)):
        start = pl.multiple_of(qi * tq + d * tk, tk)
        acc, carry = block(start, acc, carry, (scol + d * tk) < trow)

    def body(it, state):
        acc, carry = state
        for d in range(nsub):
            kb = (qi - it) * nsub - 1 - d
            acc, carry = block(pl.multiple_of(kb * tk, tk), acc, carry, None)
        return acc, carry

    acc, carry = lax.fori_loop(0, qi, body, (acc, carry))
    o_ref[...] = acc.astype(o_ref.dtype)


def _sb_prompt(proj, bias, *, batch, seq, tq, tk):
    nq = seq // tq
    grid_spec = pltpu.PrefetchScalarGridSpec(
        num_scalar_prefetch=1,
        grid=(batch, SB_HEADS, nq),
        in_specs=[pl.BlockSpec((tq, SB_HEAD_DIM), lambda b, h, i, s: (b * nq + i, h)),
                  pl.BlockSpec((seq, SB_HEAD_DIM), lambda b, h, i, s: (b, SB_HEADS + h)),
                  pl.BlockSpec((seq, SB_HEAD_DIM), lambda b, h, i, s: (b, 2 * SB_HEADS + h))],
        out_specs=pl.BlockSpec((tq, SB_HEAD_DIM), lambda b, h, i, s: (b * nq + i, h)),
    )
    return pl.pallas_call(
        functools.partial(_sb_prompt_kernel, tq=tq, tk=tk),
        grid_spec=grid_spec,
        out_shape=jax.ShapeDtypeStruct((batch * seq, SB_WIDTH), BF16),
        compiler_params=_cp("parallel", "parallel", "arbitrary"),
        name="sb_prompt",
    )(bias, proj, proj, proj)


def _sb_sample_kernel(pt_ref, q_ref, bias_ref, kn_ref, vn_ref, *rest, pages):
    k_refs, v_refs = rest[:pages], rest[pages:2 * pages]
    o_ref, acc_ref, carry_ref = rest[2 * pages:]
    s = pl.program_id(1)
    n = LANES
    q = q_ref[...]
    row = lax.broadcasted_iota(jnp.int32, (n, n), 0)
    col = lax.broadcasted_iota(jnp.int32, (n, n), 1)
    tri = jnp.where(col > row, 1.0, 0.0).astype(BF16)
    col_head = col // SUBLANES
    col_t = col % SUBLANES
    bias = bias_ref[...]

    def head_rows(ref):
        return jnp.concatenate([ref[pl.ds(h, PAGE_SIZE, stride=SB_HEADS), :].astype(BF16)
                                for h in range(SB_HEADS)], axis=0)

    def attend(k_list, v_list, mask, acc, carry):
        ps = PAGE_SIZE
        zs = []
        for k_ref in k_list:
            zz = _dot_nt(head_rows(k_ref), q)
            z = zz[0:ps, :]
            for h in range(1, SB_HEADS):
                z = jnp.where(col_head == h, zz[h * ps:(h + 1) * ps, :], z)
            zs.append(z)
        z = jnp.concatenate(zs, axis=0) + bias
        sp = jnp.log(1.0 + jnp.exp(-jnp.abs(z)))
        log_beta = jnp.minimum(z, 0.0) - sp
        log_keep = jnp.minimum(-z, 0.0) - sp
        if mask is not None:
            log_keep = jnp.where(mask, log_keep, 0.0)
        hi, lo = _split_bf16(log_keep)
        np_ = len(k_list)
        rhs = jnp.concatenate([x[p * ps:(p + 1) * ps, :] for p in range(np_) for x in (hi, lo)], axis=1)
        both = _dot(tri, rhs)
        ats = []
        for p in range(np_):
            rs = slice(p * ps, (p + 1) * ps)
            tail = both[:, 2 * p * n:(2 * p + 1) * n] + both[:, (2 * p + 1) * n:(2 * p + 2) * n]
            a = jnp.exp(log_beta[rs, :] + tail + carry)
            if mask is not None:
                a = jnp.where(mask, a, 0.0)
            carry = carry + jnp.sum(log_keep[rs, :], axis=0, keepdims=True)
            ats.append(a.T.astype(BF16))
        at = jnp.concatenate(ats, axis=1)
        vs = [head_rows(v_ref) for v_ref in v_list]
        outs = []
        for h in range(SB_HEADS):
            vh = jnp.concatenate([v[h * ps:(h + 1) * ps, :] for v in vs], axis=0)
            outs.append(_dot(at[h * SUBLANES:(h + 1) * SUBLANES, :], vh))
        return acc + jnp.concatenate(outs, axis=0), carry

    @pl.when(s == 0)
    def _():
        acc0, carry0 = attend([kn_ref], [vn_ref], row < col_t,
                              jnp.zeros(acc_ref.shape, F32), jnp.zeros((1, n), F32))
        acc_ref[...] = acc0
        carry_ref[...] = carry0

    acc, carry = attend(k_refs, v_refs, None, acc_ref[...], carry_ref[...])
    acc_ref[...] = acc
    carry_ref[...] = carry

    @pl.when(s == pl.num_programs(1) - 1)
    def _():
        o_ref[...] = acc


def _sb_sample(q_rows, bias_cols, k_new, v_new, cache_k, cache_v, page_table, *, pages):
    nb, n_pages = page_table.shape
    rows = PAGE_SIZE * SB_HEADS
    out_rows = SB_HEADS * SUBLANES

    def page_spec(p):
        return pl.BlockSpec((None, rows, SB_HEAD_DIM),
                            lambda b, s, pt: (pt[b, n_pages - 1 - (s * pages + p)], 0, 0))

    per_b = lambda b, s, pt: (b, 0, 0)
    grid_spec = pltpu.PrefetchScalarGridSpec(
        num_scalar_prefetch=1,
        grid=(nb, n_pages // pages),
        in_specs=[pl.BlockSpec((None, LANES, SB_HEAD_DIM), per_b),
                  pl.BlockSpec((1, LANES), lambda b, s, pt: (0, 0)),
                  pl.BlockSpec((None, rows, SB_HEAD_DIM), per_b),
                  pl.BlockSpec((None, rows, SB_HEAD_DIM), per_b)]
        + [page_spec(p) for p in range(pages)] * 2,
        out_specs=pl.BlockSpec((None, out_rows, SB_HEAD_DIM), per_b),
        scratch_shapes=[pltpu.VMEM((out_rows, SB_HEAD_DIM), F32), pltpu.VMEM((1, LANES), F32)],
    )
    return pl.pallas_call(
        functools.partial(_sb_sample_kernel, pages=pages),
        grid_spec=grid_spec,
        out_shape=jax.ShapeDtypeStruct((nb, out_rows, SB_HEAD_DIM), F32),
        compiler_params=_cp("parallel", "arbitrary"),
        name="sb_sample",
    )(page_table, q_rows, bias_cols, k_new, v_new, *([cache_k] * pages), *([cache_v] * pages))


def _gla_kernel(q_ref, k_ref, v_ref, g_ref, low_ref, wa_ref, ba_ref, nw_ref, s0_ref, o_ref, sout_ref, st_ref,
                *, rows, nsteps):
    si = pl.program_id(1)
    c = GLA_CHUNK
    rp = max(rows, c)
    chunks = rp // c

    @pl.when(si == 0)
    def _():
        st_ref[...] = s0_ref[...]

    def pad(x):
        if rows == rp:
            return x
        return jnp.concatenate([x, jnp.zeros((rp - rows, x.shape[1]), x.dtype)], axis=0)

    r = lax.broadcasted_iota(jnp.int32, (rp, rp), 0)
    cc = lax.broadcasted_iota(jnp.int32, (rp, rp), 1)
    causal = (r >= cc) & ((r // c) == (cc // c))
    log_a = _log_sigmoid(_dot(pad(low_ref[...]).astype(BF16), wa_ref[...]) + ba_ref[...])
    log_a = log_a * (1.0 / GLA_GATE_NORMALIZER)
    if rows < rp:
        log_a = jnp.where(lax.broadcasted_iota(jnp.int32, log_a.shape, 0) < rows, log_a, 0.0)
    hi, lo = _split_bf16(log_a)
    ltri = jnp.where(causal, 1.0, 0.0).astype(BF16)
    b = _dot(ltri, hi) + _dot(ltri, lo)
    b_last = [b[(ci + 1) * c - 1:(ci + 1) * c, :] for ci in range(chunks)]
    b_end = jnp.concatenate([jnp.broadcast_to(bl, (c, bl.shape[1])) for bl in b_last], axis=0)
    k = pad(k_ref[...])
    q_dec = (pad(q_ref[...]) * (GLA_DK ** -0.5) * jnp.exp(b)).astype(BF16)
    k_inv = (k * jnp.exp(-b)).astype(BF16)
    k_state = (k * jnp.exp(b_end - b)).astype(BF16)
    v = pad(v_ref[...]).astype(BF16)
    g = pad(g_ref[...])
    outs = []
    for h in range(GLA_HEADS):
        kc = slice(h * GLA_DK, (h + 1) * GLA_DK)
        vc = slice(h * GLA_DV, (h + 1) * GLA_DV)
        scores = _dot_nt(q_dec[:, kc], k_inv[:, kc])
        o_intra = _dot(jnp.where(causal, scores, 0.0).astype(BF16), v[:, vc])
        st = st_ref[h]
        o_inter = []
        for ci in range(chunks):
            rs = slice(ci * c, (ci + 1) * c)
            o_inter.append(_dot_nt(q_dec[rs, kc], st.astype(BF16)))
            st = st * jnp.exp(b_last[ci][:, kc]) + lax.dot_general(v[rs, vc], k_state[rs, kc], TN_DIMS,
                                                                   preferred_element_type=F32)
        st_ref[h] = st
        o = o_intra + jnp.concatenate(o_inter, axis=0)
        o = o * lax.rsqrt(jnp.mean(o * o, axis=-1, keepdims=True) + GLA_NORM_EPS) * nw_ref[...]
        gh = g[:, vc]
        outs.append(o * (gh * _sigmoid(gh)))
    o_ref[...] = jnp.concatenate(outs, axis=1)[:rows, :].astype(o_ref.dtype)

    @pl.when(si == nsteps - 1)
    def _():
        sout_ref[...] = st_ref[...]


def _gla(proj, low, wa, ba, nw, s0t, *, batch, seq, rows):
    nsteps = seq // rows
    qc = 3 * SB_WIDTH // GLA_K_WIDTH
    vc = (3 * SB_WIDTH + 2 * GLA_K_WIDTH) // GLA_V_WIDTH
    rowblk = lambda b, i: b * nsteps + i
    st = pl.BlockSpec((None, GLA_HEADS, GLA_DV, GLA_DK), lambda b, i: (b, 0, 0, 0))
    return pl.pallas_call(
        functools.partial(_gla_kernel, rows=rows, nsteps=nsteps),
        grid=(batch, nsteps),
        in_specs=[pl.BlockSpec((rows, GLA_K_WIDTH), lambda b, i: (rowblk(b, i), qc)),
                  pl.BlockSpec((rows, GLA_K_WIDTH), lambda b, i: (rowblk(b, i), qc + 1)),
                  pl.BlockSpec((rows, GLA_V_WIDTH), lambda b, i: (rowblk(b, i), vc)),
                  pl.BlockSpec((rows, GLA_V_WIDTH), lambda b, i: (rowblk(b, i), vc + 1)),
                  pl.BlockSpec((rows, GLA_GATE_PAD), lambda b, i: (rowblk(b, i), 0)),
                  pl.BlockSpec((GLA_GATE_PAD, GLA_K_WIDTH), lambda b, i: (0, 0)),
                  pl.BlockSpec((1, GLA_K_WIDTH), lambda b, i: (0, 0)),
                  pl.BlockSpec((1, GLA_DV), lambda b, i: (0, 0)),
                  st],
        out_specs=[pl.BlockSpec((rows, GLA_V_WIDTH), lambda b, i: (rowblk(b, i), 0)), st],
        out_shape=[jax.ShapeDtypeStruct((batch * seq, GLA_V_WIDTH), _row_dtype(rows)),
                   jax.ShapeDtypeStruct((batch, GLA_HEADS, GLA_DV, GLA_DK), F32)],
        scratch_shapes=[pltpu.VMEM((GLA_HEADS, GLA_DV, GLA_DK), F32)],
        compiler_params=_cp("parallel", "arbitrary"),
        name="gla",
    )(proj, proj, proj, proj, low, wa, ba.reshape(1, -1), nw.reshape(1, -1), s0t)


def _rw_mix_kernel(x_ref, xp_ref, mix_ref, *o_refs):
    x = x_ref[...]
    xx = xp_ref[...] - x
    for m, o_ref in enumerate(o_refs):
        o_ref[...] = (x + xx * mix_ref[m:m + 1, :]).astype(o_ref.dtype)


def _rw_mix(x, x_prev, mix, *, tm):
    m, d = x.shape
    row = pl.BlockSpec((tm, d), lambda i: (i, 0))
    return pl.pallas_call(
        _rw_mix_kernel,
        grid=(m // tm,),
        in_specs=[row, row, pl.BlockSpec((6, d), lambda i: (0, 0))],
        out_specs=[row] * 6,
        out_shape=[jax.ShapeDtypeStruct((m, d), BF16)] * 6,
        compiler_params=_cp("parallel"),
        name="rw_mix",
    )(x, x_prev, mix)


def _store_lane_rows(o_ref, val):
    rows, width = val.shape
    n = width // LANES
    for g in range(n):
        o_ref[pl.ds(g, rows, stride=n), :] = val[:, g * LANES:(g + 1) * LANES]


def _rw_lora_kernel(xw_ref, xa_ref, xg_ref, w1_ref, w2_ref, a1_ref, a2_ref, g1_ref, g2_ref, w0_ref, a0_ref,
                    dec_ref, a_ref, g_ref):
    hw = jnp.tanh(_dot(xw_ref[...], w1_ref[...]))
    wl = w0_ref[...] + _dot(hw.astype(BF16), w2_ref[...])
    w_log = _log_sigmoid(wl) - 0.5
    _store_lane_rows(dec_ref, jnp.exp(-jnp.exp(w_log)))
    ha = _dot(xa_ref[...], a1_ref[...])
    _store_lane_rows(a_ref, _sigmoid(a0_ref[...] + _dot(ha.astype(BF16), a2_ref[...])))
    hg = _sigmoid(_dot(xg_ref[...], g1_ref[...]))
    g_ref[...] = _dot(hg.astype(BF16), g2_ref[...])


def _rw_lora(xw, xa, xg, w1, w2, a1, a2, g1, g2, w0, a0, *, tm):
    m, d = xw.shape
    ng = d // LANES
    row = pl.BlockSpec((tm, d), lambda i: (i, 0))
    lane_rows = pl.BlockSpec((tm * ng, LANES), lambda i: (i, 0))
    full = lambda a: pl.BlockSpec(a.shape, lambda i: (0, 0))
    w0 = w0.reshape(1, d)
    a0 = a0.reshape(1, d)
    consts = [w1, w2, a1, a2, g1, g2, w0, a0]
    return pl.pallas_call(
        _rw_lora_kernel,
        grid=(m // tm,),
        in_specs=[row, row, row] + [full(a) for a in consts],
        out_specs=[lane_rows, lane_rows, row],
        out_shape=[jax.ShapeDtypeStruct((m * ng, LANES), F32)] * 2 + [jax.ShapeDtypeStruct((m, d), F32)],
        compiler_params=_cp("parallel"),
        name="rw_lora",
    )(xw, xa, xg, *consts)


def _mm_lane_rows_kernel(x_ref, w_ref, o_ref):
    _store_lane_rows(o_ref, _dot(x_ref[...], w_ref[...]))


def _mm_lane_rows(x, w, lead, *, tm):
    m, k = x.shape
    n = w.shape[2]
    ng = n // LANES
    return pl.pallas_call(
        _mm_lane_rows_kernel,
        grid=(m // tm,),
        in_specs=[pl.BlockSpec((tm, k), lambda i: (i, 0)),
                  pl.BlockSpec((None, k, n), lambda i: (lead, 0, 0))],
        out_specs=pl.BlockSpec((tm * ng, LANES), lambda i: (i, 0)),
        out_shape=jax.ShapeDtypeStruct((m * ng, LANES), F32),
        compiler_params=_cp("parallel"),
        name="mm_lane_rows",
    )(x, w)


def _lane_group_sum(x):
    axis = x.ndim - 1
    x = x + pltpu.roll(x, RW_HEADS, axis=axis)
    return x + pltpu.roll(x, 2 * RW_HEADS, axis=axis)


def _rw_scan_kernel(r_ref, w_ref, k_ref, a_ref, v_ref, kkw_ref, kaw_ref, rkw_ref, lnw_ref, lnb_ref, s0_ref,
                    y_ref, sout_ref, s00_ref, s01_ref, s10_ref, s11_ref, kk_ref, bb_ref, km_ref, bonus_ref, vt_ref,
                    *, tt, nblk):
    tb = pl.program_id(1)
    ng = D_MODEL // LANES
    nh = RW_HEAD // 2
    halves = (slice(0, nh), slice(nh, RW_HEAD))
    state = ((s00_ref, s01_ref), (s10_ref, s11_ref))

    @pl.when(tb == 0)
    def _():
        for b in range(2):
            for h in range(2):
                state[b][h][...] = s0_ref[b, :, halves[h], :]

    def head_total(x):
        return _lane_group_sum(jnp.sum(x, axis=2, keepdims=True))

    for c0 in range(0, tt, SUBLANES):
        ts = slice(c0, min(c0 + SUBLANES, tt))
        k = k_ref[:, ts]
        a = a_ref[:, ts]
        kk = k * kkw_ref[...]
        kk = kk * lax.rsqrt(jnp.maximum(head_total(kk * kk), 1e-24))
        km = k * (1.0 + (a - 1.0) * kaw_ref[...])
        kk_ref[:, ts] = kk
        bb_ref[:, ts] = kk * a
        km_ref[:, ts] = km
        bonus_ref[:, ts] = head_total(r_ref[:, ts] * km * rkw_ref[...])

    rc = lax.broadcasted_iota(jnp.int32, (LANES, 4 * LANES), 0)
    cc = lax.broadcasted_iota(jnp.int32, (LANES, 4 * LANES), 1)
    spread = jnp.where((cc // LANES == rc // RW_HEADS) & (cc % RW_HEADS == rc % RW_HEADS), 1.0, 0.0).astype(BF16)
    for b in range(2):
        x = v_ref[b].reshape(tt * ng, LANES)
        hi = x.astype(BF16)
        rest = x - hi.astype(F32)
        mid = rest.astype(BF16)
        lo = (rest - mid.astype(F32)).astype(BF16)
        rep = _dot(hi, spread) + _dot(mid, spread) + _dot(lo, spread)
        for jv in range(4):
            vt_ref[b, pl.ds(jv, tt * ng, stride=4), :] = rep[:, jv * LANES:(jv + 1) * LANES]

    def value_rows(b, t, h):
        return vt_ref[b, pl.ds(pl.multiple_of(t * RW_HEAD + h * nh, nh), nh), :]

    def emit(b, t, y_parts):
        ys = [_lane_group_sum(p) for p in y_parts]
        mu = (jnp.sum(ys[0], axis=0, keepdims=True) + jnp.sum(ys[1], axis=0, keepdims=True)) * (1.0 / RW_HEAD)
        ds = [y - mu for y in ys]
        var = (jnp.sum(ds[0] * ds[0], axis=0, keepdims=True)
               + jnp.sum(ds[1] * ds[1], axis=0, keepdims=True)) * (1.0 / RW_HEAD)
        inv = lax.rsqrt(var + RW_LN_EPS)
        for h, (rows, d) in enumerate(zip(halves, ds)):
            y_ref[b, t, rows, :] = (d * inv * lnw_ref[rows, :] + lnb_ref[rows, :]
                                    + bonus_ref[b, t] * value_rows(b, t, h))

    def update(b, h, t, t_next, sa):
        st = state[b][h]
        vv = value_rows(b, t, h)
        y_acc = jnp.zeros((nh, LANES), F32)
        sa_acc = jnp.zeros((nh, LANES), F32)
        for g in range(ng):
            sg = st[g] * w_ref[b, t, g:g + 1, :] + sa * bb_ref[b, t, g:g + 1, :] + vv * km_ref[b, t, g:g + 1, :]
            st[g] = sg
            y_acc = y_acc + sg * r_ref[b, t, g:g + 1, :]
            sa_acc = sa_acc - sg * kk_ref[b, t_next, g:g + 1, :]
        return sa_acc, y_acc

    def first_dot(b, h):
        acc = jnp.zeros((nh, LANES), F32)
        for g in range(ng):
            acc = acc - state[b][h][g] * kk_ref[b, 0, g:g + 1, :]
        return acc

    def step(t, carry):
        sa00, sa01, sa10_parts, sa11_parts, y10_parts, y11_parts = carry
        t_next = jnp.minimum(t + 1, tt - 1)
        sa10 = _lane_group_sum(sa10_parts)
        sa11 = _lane_group_sum(sa11_parts)
        emit(1, jnp.maximum(t - 1, 0), (y10_parts, y11_parts))
        sa00_parts, y00_parts = update(0, 0, t, t_next, sa00)
        sa01_parts, y01_parts = update(0, 1, t, t_next, sa01)
        sa00 = _lane_group_sum(sa00_parts)
        sa10_parts, y10_parts = update(1, 0, t, t_next, sa10)
        sa01 = _lane_group_sum(sa01_parts)
        emit(0, t, (y00_parts, y01_parts))
        sa11_parts, y11_parts = update(1, 1, t, t_next, sa11)
        return sa00, sa01, sa10_parts, sa11_parts, y10_parts, y11_parts

    zero = jnp.zeros((nh, LANES), F32)
    init = (_lane_group_sum(first_dot(0, 0)), _lane_group_sum(first_dot(0, 1)),
            first_dot(1, 0), first_dot(1, 1), zero, zero)
    last = lax.fori_loop(0, tt, step, init)
    emit(1, tt - 1, last[4:])

    @pl.when(tb == nblk - 1)
    def _():
        for b in range(2):
            for h in range(2):
                sout_ref[b, :, halves[h], :] = state[b][h][...]


def _rw_scan(r, w, k, a, v, kkw, kaw, rkw, lnw, lnb, s0, *, tt):
    b, t, ng, _ = r.shape
    nblk = t // tt
    kblk = pl.BlockSpec((2, tt, ng, LANES), lambda gi, ti: (gi, ti, 0, 0))
    vblk = pl.BlockSpec((2, tt, RW_HEAD, LANES), lambda gi, ti: (gi, ti, 0, 0))
    kconst = pl.BlockSpec((ng, LANES), lambda gi, ti: (0, 0))
    vconst = pl.BlockSpec((RW_HEAD, LANES), lambda gi, ti: (0, 0))
    sblk = pl.BlockSpec((2, ng, RW_HEAD, LANES), lambda gi, ti: (gi, 0, 0, 0))
    return pl.pallas_call(
        functools.partial(_rw_scan_kernel, tt=tt, nblk=nblk),
        grid=(b // 2, nblk),
        in_specs=[kblk, kblk, kblk, kblk, kblk, kconst, kconst, kconst, vconst, vconst, sblk],
        out_specs=[vblk, sblk],
        out_shape=[jax.ShapeDtypeStruct((b, t, RW_HEAD, LANES), F32),
                   jax.ShapeDtypeStruct((b, ng, RW_HEAD, LANES), F32)],
        scratch_shapes=([pltpu.VMEM((ng, RW_HEAD // 2, LANES), F32)] * 4 + [pltpu.VMEM((2, tt, ng, LANES), F32)] * 3
                        + [pltpu.VMEM((2, tt, 1, LANES), F32), pltpu.VMEM((2, tt * RW_HEAD, LANES), F32)]),
        compiler_params=_cp("parallel", "arbitrary"),
        name="rw_scan",
    )(r, w, k, a, v, kkw, kaw, rkw, lnw, lnb, s0)


def _head_minor(vec):
    return vec.reshape(RW_HEADS, RW_HEAD).T.reshape(1, D_MODEL)


def _head_minor_matrix():
    new = jnp.arange(D_MODEL)
    old = (new % RW_HEADS) * RW_HEAD + new // RW_HEADS
    return (jnp.arange(D_MODEL)[:, None] == old[None, :]).astype(BF16)[None]


def _value_tiles(vec):
    lead = vec.shape[:-1]
    x = vec.reshape(*lead, RW_HEAD, 1, RW_HEADS)
    return jnp.broadcast_to(x, (*lead, RW_HEAD, LANES // RW_HEADS, RW_HEADS)).reshape(*lead, RW_HEAD, LANES)


def _state_to_scan(s):
    bsz = s.shape[0]
    s = s.reshape(bsz, RW_HEADS, RW_HEAD, RW_HEAD // 4, 4).transpose(0, 3, 2, 4, 1)
    return s.reshape(bsz, RW_HEAD // 4, RW_HEAD, LANES)


def _state_from_scan(s):
    bsz = s.shape[0]
    s = s.reshape(bsz, RW_HEAD // 4, RW_HEAD, 4, RW_HEADS).transpose(0, 4, 2, 1, 3)
    return s.reshape(bsz, RW_HEADS, RW_HEAD, RW_HEAD)


def _prep_weights(p):
    w = {}
    w["mx_w_in"] = p["mx_w_in"]
    gpad = GLA_GATE_PAD - GLA_GATE_RANK
    w["mx_w_low"] = jnp.pad(p["mx_w_in"][:, :, EVEN_MAIN_WIDTH:], ((0, 0), (0, 0), (0, gpad))).astype(BF16)
    w["gla_w_a2"] = jnp.pad(p["gla_w_a2"], ((0, 0), (0, gpad), (0, 0))).astype(BF16)
    w["mx_w_out"] = p["mx_w_out"].astype(BF16)
    n_odd = p["rw_w_rkv"].shape[0]
    perm = _head_minor_matrix()
    permute = lambda rows: _mm(rows, perm, 0, D_MODEL, tm=min(rows.shape[0], 1024), tn=1024, out_dtype=BF16)
    w["rw_w_rkv"] = permute(p["rw_w_rkv"].reshape(-1, D_MODEL)).reshape(-1, D_MODEL, D_MODEL)
    pad_c = lambda a: jnp.pad(a, ((0, 0), (0, 0), (0, RW_RANK_PAD - a.shape[2]))).astype(BF16)
    pad_r = lambda a: jnp.pad(a, ((0, 0), (0, RW_RANK_PAD - a.shape[1]), (0, 0)))
    second = jnp.concatenate([pad_r(p["rw_w2"]), pad_r(p["rw_a2"]), p["rw_g2"]], axis=1)
    second = permute(second.reshape(-1, D_MODEL)).reshape(n_odd, -1, D_MODEL)
    w["rw_w1"], w["rw_a1"], w["rw_g1"] = pad_c(p["rw_w1"]), pad_c(p["rw_a1"]), p["rw_g1"].astype(BF16)
    w["rw_w2"] = second[:, :RW_RANK_PAD]
    w["rw_a2"] = second[:, RW_RANK_PAD:2 * RW_RANK_PAD]
    w["rw_g2"] = second[:, 2 * RW_RANK_PAD:]
    w["rw_w_o"] = (p["rw_w_o"].reshape(n_odd, RW_HEADS, RW_HEAD, D_MODEL).transpose(0, 2, 1, 3)
                   .reshape(n_odd, D_MODEL, D_MODEL).astype(BF16))
    for name in ("rw_w0", "rw_a0", "rw_k_k", "rw_k_a"):
        w[name] = [_head_minor(p[name][j]) for j in range(n_odd)]
    w["rw_r_k"] = [_head_minor(p["rw_r_k"][j].reshape(-1)) for j in range(n_odd)]
    for name in ("rw_ln_w", "rw_ln_b"):
        w[name] = [_value_tiles(_head_minor(p[name][j])[0]) for j in range(n_odd)]
    w["ca_w_q"] = p["ca_w_q"].astype(BF16)
    w["ca_w_kv"] = p["ca_w_kv"]
    w["ca_w_o"] = p["ca_w_o"].astype(BF16)
    w["ffn_w_up"] = p["ffn_w_up"]
    w["ffn_conv_b"] = p["ffn_conv_b"][:, None, :]
    w["ffn_down"] = p["ffn_w_down"].astype(BF16)
    for name in ("sb_bias", "gla_b_a", "gla_norm_w", "rw_mix", "ffn_conv_w", "ln_w", "ln_b"):
        w[name] = p[name]
    return w


def _trunk(x, mem_k, mem_v, mem_cols, sb_past, page_table, gla_s0, rw_s0, rw_shift0, conv0, w, *, prompt):
    bsz, seq, _ = x.shape
    m = bsz * seq
    tm = 512 if prompt else m
    x = x.reshape(m, D_MODEL)
    new_k, new_v, new_gla, new_rw, new_shift, new_conv = [], [], [], [], [], []
    for i in range(DEPTH):
        j = i // 2
        lw, lb = w["ln_w"][i], w["ln_b"][i]
        if i % 2 == 0:
            proj = _mm(x, w["mx_w_in"], j, EVEN_MAIN_WIDTH, tm=min(m, 1024), tn=768)
            low = _mm(x, w["mx_w_low"], j, GLA_GATE_PAD, tm=min(m, 1024), tn=GLA_GATE_PAD)
            ka = proj[:, SB_WIDTH:2 * SB_WIDTH].reshape(bsz, seq, SB_HEADS, SB_HEAD_DIM)
            va = proj[:, 2 * SB_WIDTH:3 * SB_WIDTH].reshape(bsz, seq, SB_HEADS, SB_HEAD_DIM)
            new_k.append(ka)
            new_v.append(va)
            if prompt:
                oa = _sb_prompt(proj, w["sb_bias"][j], batch=bsz, seq=seq, tq=1024, tk=256)
            else:
                cache_k, cache_v = sb_past
                n_phys = cache_k.shape[1]
                rows = PAGE_SIZE * SB_HEADS
                qa = proj[:, :SB_WIDTH].reshape(bsz, seq, SB_HEADS, SB_HEAD_DIM).transpose(0, 2, 1, 3)
                qa = (qa.reshape(bsz, SB_HEADS * seq, SB_HEAD_DIM) * (SB_HEAD_DIM ** -0.5)).astype(BF16)
                qa = jnp.pad(qa, ((0, 0), (0, LANES - SB_HEADS * seq), (0, 0)))
                bias_cols = jnp.pad(jnp.repeat(w["sb_bias"][j], seq), (0, LANES - SB_HEADS * seq)).reshape(1, LANES)
                pad_new = lambda a: jnp.pad(a.reshape(bsz, seq * SB_HEADS, SB_HEAD_DIM),
                                            ((0, 0), (0, rows - seq * SB_HEADS), (0, 0)))
                oa = _sb_sample(qa, bias_cols, pad_new(ka), pad_new(va),
                                cache_k[j].reshape(n_phys, rows, SB_HEAD_DIM),
                                cache_v[j].reshape(n_phys, rows, SB_HEAD_DIM), page_table, pages=8)
                oa = oa[:, :SB_HEADS * seq].reshape(bsz, SB_HEADS, seq, SB_HEAD_DIM).transpose(0, 2, 1, 3)
                oa = oa.reshape(m, SB_WIDTH).astype(BF16)
            ob, s_t = _gla(proj, low, w["gla_w_a2"][j], w["gla_b_a"][j], w["gla_norm_w"][j],
                           gla_s0[j].transpose(0, 1, 3, 2), batch=bsz, seq=seq, rows=min(seq, 256))
            new_gla.append(s_t.transpose(0, 1, 3, 2))
            o = jnp.concatenate([oa, ob.astype(BF16)], axis=-1)
            x = _mm_ln(o, w["mx_w_out"][j], x, lw[0], lb[0], tm=tm, tk=o.shape[1])
        else:
            x3 = x.reshape(bsz, seq, D_MODEL)
            new_shift.append(x3[:, -1])
            x_prev = jnp.concatenate([rw_shift0[j][:, None, :], x3[:, :-1]], axis=1).reshape(m, D_MODEL)
            xr, xw, xk, xv, xa, xg = _rw_mix(x, x_prev, w["rw_mix"][j], tm=tm)
            r = _mm_lane_rows(xr, w["rw_w_rkv"], 3 * j, tm=min(m, 512))
            k = _mm_lane_rows(xk, w["rw_w_rkv"], 3 * j + 1, tm=min(m, 512))
            v = _mm_lane_rows(xv, w["rw_w_rkv"], 3 * j + 2, tm=min(m, 512))
            dec, a, gate = _rw_lora(xw, xa, xg, w["rw_w1"][j], w["rw_w2"][j], w["rw_a1"][j], w["rw_a2"][j],
                                    w["rw_g1"][j], w["rw_g2"][j], w["rw_w0"][j], w["rw_a0"][j], tm=min(m, 256))
            rows = lambda z: z.reshape(bsz, seq, D_MODEL // LANES, LANES)
            krow = lambda z: z.reshape(D_MODEL // LANES, LANES)
            y, s_new = _rw_scan(rows(r), rows(dec), rows(k), rows(a), rows(v),
                                krow(w["rw_k_k"][j]), krow(w["rw_k_a"][j]), krow(w["rw_r_k"][j]),
                                w["rw_ln_w"][j], w["rw_ln_b"][j],
                                _state_to_scan(rw_s0[j]), tt=min(seq, 32))
            new_rw.append(_state_from_scan(s_new))
            y = y[..., :RW_HEADS].reshape(m, D_MODEL)
            x = _mm_ln(y, w["rw_w_o"][j], x, lw[0], lb[0], tm=tm, tk=D_MODEL, gate=gate)
        x, xb = _cross_attn(x, mem_k[i], mem_v[i], mem_cols[0], mem_cols[1], w["ca_w_q"][i], w["ca_w_o"][i],
                            lw[1], lb[1], batch=bsz, tm=min(seq, 512))
        f2 = 2 * FFN_HIDDEN
        cw, cb = w["ffn_conv_w"][i], w["ffn_conv_b"][i]
        if prompt:
            c0 = jnp.pad(conv0[i], ((0, 0), (SUBLANES - conv0[i].shape[1], 0), (0, 0)))
            c, cs = _ffn_up(xb, w["ffn_w_up"], i, cw, cb, c0, tm=min(seq, 2048), tn=256, shift=1, seqs=bsz)
            new_conv.append(cs[:, -2:, :])
        else:
            c0 = conv0[i].transpose(1, 0, 2).reshape(1, 2 * bsz, f2)
            xt = xb.reshape(bsz, seq, D_MODEL).transpose(1, 0, 2).reshape(m, D_MODEL)
            c, cs = _ffn_up(xt, w["ffn_w_up"], i, cw, cb, c0, tm=m, tn=256, shift=bsz, seqs=1)
            c = c.reshape(seq, bsz, f2).transpose(1, 0, 2).reshape(m, f2)
            new_conv.append(cs.reshape(2, bsz, f2).transpose(1, 0, 2))
        x = _glu_ln(c, w["ffn_down"], i, x, lw[2], lb[2], tm=min(m, 256), kchunk=1408)
    return (x.reshape(bsz, seq, D_MODEL), jnp.stack(new_k), jnp.stack(new_v), jnp.stack(new_gla),
            jnp.stack(new_rw), jnp.stack(new_shift), jnp.stack(new_conv))


def kernel(x_prompt, x_sample, mem_prompt, cache_sb_k, cache_sb_v, page_table, state_gla, state_rwkv,
           state_rwkv_shift, state_ffn_conv, cache_mem_k, cache_mem_v, mx_w_in, sb_bias, gla_w_a2, gla_b_a,
           gla_norm_w, mx_w_out, rw_mix, rw_w_rkv, rw_w0, rw_w1, rw_w2, rw_a0, rw_a1, rw_a2, rw_g1, rw_g2,
           rw_k_k, rw_k_a, rw_r_k, rw_ln_w, rw_ln_b, rw_w_o, ca_w_q, ca_w_kv, ca_w_o, ffn_w_up, ffn_conv_w,
           ffn_conv_b, ffn_w_down, ln_w, ln_b):
    w = _prep_weights(dict(
        mx_w_in=mx_w_in, sb_bias=sb_bias, gla_w_a2=gla_w_a2, gla_b_a=gla_b_a, gla_norm_w=gla_norm_w,
        mx_w_out=mx_w_out, rw_mix=rw_mix, rw_w_rkv=rw_w_rkv, rw_w0=rw_w0, rw_w1=rw_w1, rw_w2=rw_w2,
        rw_a0=rw_a0, rw_a1=rw_a1, rw_a2=rw_a2, rw_g1=rw_g1, rw_g2=rw_g2, rw_k_k=rw_k_k, rw_k_a=rw_k_a,
        rw_r_k=rw_r_k, rw_ln_w=rw_ln_w, rw_ln_b=rw_ln_b, rw_w_o=rw_w_o, ca_w_q=ca_w_q, ca_w_kv=ca_w_kv,
        ca_w_o=ca_w_o, ffn_w_up=ffn_w_up, ffn_conv_w=ffn_conv_w, ffn_conv_b=ffn_conv_b,
        ffn_w_down=ffn_w_down, ln_w=ln_w, ln_b=ln_b))
    n_even = (DEPTH + 1) // 2
    n_odd = DEPTH // 2
    b, _, _ = x_prompt.shape
    db = x_sample.shape[0]

    mem_rows = mem_prompt.reshape(b * N_MEM, D_MODEL)
    mem_kv = [_mm(mem_rows, w["ca_w_kv"], i, 2 * MEM_WIDTH, tm=b * N_MEM, tn=512).reshape(b, N_MEM, 2 * MEM_WIDTH)
              for i in range(DEPTH)]
    mem_shape = (DEPTH, b, N_MEM, MEM_HEADS, MEM_HEAD_DIM)
    mem_k_p = jnp.stack([kv[..., :MEM_WIDTH] for kv in mem_kv]).reshape(mem_shape)
    mem_v_p = jnp.stack([kv[..., MEM_WIDTH:] for kv in mem_kv]).reshape(mem_shape)
    gla0 = jnp.zeros((n_even, b, GLA_HEADS, GLA_DK, GLA_DV), F32)
    rw0 = jnp.zeros((n_odd, b, RW_HEADS, RW_HEAD, RW_HEAD), F32)
    sh0 = jnp.zeros((n_odd, b, D_MODEL), F32)
    cv0 = jnp.zeros((DEPTH, b, 2, 2 * FFN_HIDDEN), F32)
    y_p, sbk_p, sbv_p, gla_p, rw_p, sh_p, conv_p = _trunk(
        x_prompt, mem_kv, mem_kv, (0, 1), None, None, gla0, rw0, sh0, cv0, w, prompt=True)

    mk = cache_mem_k.reshape(DEPTH, db, N_MEM, MEM_WIDTH)
    mv = cache_mem_v.reshape(DEPTH, db, N_MEM, MEM_WIDTH)
    y_s, sbk_s, sbv_s, gla_s, rw_s, sh_s, conv_s = _trunk(
        x_sample, mk, mv, (0, 0), (cache_sb_k, cache_sb_v), page_table, state_gla, state_rwkv,
        state_rwkv_shift, state_ffn_conv, w, prompt=False)
    return (y_p, y_s, sbk_p, sbv_p, gla_p, rw_p, sh_p, conv_p, mem_k_p, mem_v_p,
            sbk_s, sbv_s, gla_s, rw_s, sh_s, conv_s)
```
